```python
import jax, jax.numpy as jnp
from jax import lax
import numpy as np

D_MODEL = 1024
BATCH = 16
SEQ = 4096
DEPTH = 1

GRID_W = 64
CTX_LEN = 256
EPS = 1e-6
N_MOD = 6
RET_HEADS = 4
RET_DK = 256
RET_DV = 256
RET_CHUNK = 128
ROPE_BASE = 10000.0
RET_QK_W = RET_HEADS * RET_DK
RET_V_W = RET_HEADS * RET_DV
CONV_CH = D_MODEL
CONV_WIDTH = 31
IN_WIDTH = 2 * RET_QK_W + 2 * RET_V_W + 2 * CONV_CH + 2 * D_MODEL
IN_SPLITS = (RET_QK_W, 2 * RET_QK_W, 2 * RET_QK_W + RET_V_W, 2 * RET_QK_W + 2 * RET_V_W,
             2 * RET_QK_W + 2 * RET_V_W + CONV_CH, 2 * RET_QK_W + 2 * RET_V_W + 2 * CONV_CH,
             2 * RET_QK_W + 2 * RET_V_W + 2 * CONV_CH + D_MODEL)
PEER_HEADS = 8
PEER_N_KEYS = 128
PEER_N_EXPERTS = PEER_N_KEYS * PEER_N_KEYS
PEER_DK = 256
PEER_DK_HALF = PEER_DK // 2
PEER_TOPK = 16
PEER_BLOCK = 128

kernel_name = "hybrid_retention_conformer_peer_dit_layer"


def _rms_norm(x, gain):
    x32 = x.astype(jnp.float32)
    y = x32 * lax.rsqrt(jnp.mean(x32 * x32, axis=-1, keepdims=True) + EPS)
    return (y * gain.astype(jnp.float32)).astype(x.dtype)


def _layer_norm(x, gain, bias):
    x32 = x.astype(jnp.float32)
    mu = jnp.mean(x32, axis=-1, keepdims=True)
    xc = x32 - mu
    y = xc * lax.rsqrt(jnp.mean(xc * xc, axis=-1, keepdims=True) + EPS)
    return (y * gain.astype(jnp.float32) + bias.astype(jnp.float32)).astype(x.dtype)


def _modulate(h, shift, scale):
    return h * (1.0 + scale) + shift


def _heads(t, n_heads, d_head):
    b, l, _ = t.shape
    return t.reshape(b, l, n_heads, d_head).transpose(0, 2, 1, 3)


def _axial_rotary(n_rows):
    rows = jnp.repeat(jnp.arange(n_rows, dtype=jnp.float32), GRID_W)
    cols = jnp.tile(jnp.arange(GRID_W, dtype=jnp.float32), n_rows)
    quarter = RET_DK // 4
    inv = ROPE_BASE ** (-jnp.arange(quarter, dtype=jnp.float32) / quarter)
    ang = jnp.concatenate([rows[:, None] * inv, cols[:, None] * inv], axis=-1)
    return jnp.cos(ang), jnp.sin(ang)


def _rotary(t, cos, sin):
    t1, t2 = jnp.split(t, 2, axis=-1)
    cos = cos.astype(t.dtype)
    sin = sin.astype(t.dtype)
    return jnp.concatenate([t1 * cos - t2 * sin, t1 * sin + t2 * cos], axis=-1)


def _retention_scan(q, k, v, log_gamma, state0):
    out_dtype = v.dtype
    q = q.astype(jnp.float32)
    k = k.astype(jnp.float32)
    v = v.astype(jnp.float32)
    b, h, l, dk = q.shape
    dv = v.shape[-1]
    nc = l // RET_CHUNK
    lg = log_gamma.astype(jnp.float32)
    pos = jnp.arange(RET_CHUNK, dtype=jnp.float32)
    diff = pos[:, None] - pos[None, :]
    decay_in = jnp.where(diff >= 0, jnp.exp(lg[:, None, None] * jnp.maximum(diff, 0.0)), 0.0)
    q_dec = jnp.exp(lg[:, None] * (pos + 1.0))[:, :, None]
    k_dec = jnp.exp(lg[:, None] * (RET_CHUNK - 1.0 - pos))[:, :, None]
    chunk_dec = jnp.exp(lg * RET_CHUNK)[:, None, None]

    def to_chunks(t):
        return jnp.moveaxis(t.reshape(b, h, nc, RET_CHUNK, t.shape[-1]), 2, 0)

    def step(state, inp):
        qi, ki, vi = inp
        scores = jnp.einsum('bhnd,bhmd->bhnm', qi, ki) * decay_in
        y = jnp.einsum('bhnm,bhme->bhne', scores, vi) + jnp.einsum('bhnd,bhde->bhne', qi * q_dec, state)
        state = state * chunk_dec + jnp.einsum('bhmd,bhme->bhde', ki * k_dec, vi)
        return state, y

    _, ys = lax.scan(step, state0.astype(jnp.float32), (to_chunks(q), to_chunks(k), to_chunks(v)))
    return jnp.moveaxis(ys, 0, 2).reshape(b, h, l, dv).astype(out_dtype)


def _context_states(hc, w_in, log_gamma):
    kv = hc @ w_in[:, IN_SPLITS[0]:IN_SPLITS[2]]
    k, v = jnp.split(kv, [RET_QK_W], axis=-1)
    k = (_heads(k, RET_HEADS, RET_DK) * RET_DK ** -0.5).astype(jnp.float32)
    v = _heads(v, RET_HEADS, RET_DV).astype(jnp.float32)
    l = k.shape[2]
    pos = jnp.arange(l, dtype=jnp.float32)
    lg = log_gamma.astype(jnp.float32)
    w_f = jnp.exp(lg[0][:, None] * (l - 1.0 - pos))
    w_b = jnp.exp(lg[1][:, None] * pos)
    st_f = jnp.einsum('bhld,bhle,hl->bhde', k, v, w_f)
    st_b = jnp.einsum('bhld,bhle,hl->bhde', k, v, w_b)
    return st_f, st_b


def _depthwise_conv(t, w, b):
    ch = t.shape[-1]
    y = lax.conv_general_dilated(t, w[:, None, :].astype(t.dtype), window_strides=(1,),
                                 padding=[(CONV_WIDTH // 2, CONV_WIDTH // 2)],
                                 dimension_numbers=('NWC', 'WIO', 'NWC'), feature_group_count=ch)
    return y + b


def _mixer(h, w_in, log_gamma, w_ret_o, conv_w, conv_b, conv_norm_g, conv_norm_b, w_conv_o, w_out,
           rope, state_f, state_b):
    bsz, l, _ = h.shape
    proj = h @ w_in
    q, k, v, g_ret, glu_a, glu_b, gate_a, gate_b = jnp.split(proj, IN_SPLITS, axis=-1)
    q = _heads(q, RET_HEADS, RET_DK)
    k = _heads(k, RET_HEADS, RET_DK)
    v = _heads(v, RET_HEADS, RET_DV)
    if rope is not None:
        q = _rotary(q, *rope)
        k = _rotary(k, *rope)
    k = k * RET_DK ** -0.5
    y_f = _retention_scan(q, k, v, log_gamma[0], state_f)
    y_b = _retention_scan(jnp.flip(q, axis=2), jnp.flip(k, axis=2), jnp.flip(v, axis=2), log_gamma[1], state_b)
    y = (y_f + jnp.flip(y_b, axis=2)).astype(jnp.float32)
    y = y * lax.rsqrt(jnp.mean(y * y, axis=-1, keepdims=True) + EPS)
    y = y.transpose(0, 2, 1, 3).reshape(bsz, l, RET_V_W).astype(h.dtype)
    branch_a = (y * jax.nn.silu(g_ret)) @ w_ret_o
    u = glu_a * jax.nn.sigmoid(glu_b)
    u = _depthwise_conv(u, conv_w, conv_b)
    u = jax.nn.silu(_layer_norm(u, conv_norm_g, conv_norm_b))
    branch_b = u @ w_conv_o
    merged = jax.nn.sigmoid(gate_a) * branch_a + jax.nn.sigmoid(gate_b) * branch_b
    return merged @ w_out


def _peer(h, w_query, sub_keys, expert_u, expert_v):
    bsz, l, d = h.shape
    blocks = h.reshape(bsz * l // PEER_BLOCK, PEER_BLOCK, d)

    def block(xb):
        q = (xb @ w_query).reshape(PEER_BLOCK, PEER_HEADS, 2, PEER_DK_HALF)
        s = jnp.einsum('thpd,hpnd->thpn', q, sub_keys)
        s_top, i_top = lax.top_k(s, PEER_TOPK)
        cand = s_top[:, :, 0, :, None] + s_top[:, :, 1, None, :]
        cand_idx = i_top[:, :, 0, :, None] * PEER_N_KEYS + i_top[:, :, 1, None, :]
        best, sel = lax.top_k(cand.reshape(PEER_BLOCK, PEER_HEADS, PEER_TOPK * PEER_TOPK), PEER_TOPK)
        experts = jnp.take_along_axis(cand_idx.reshape(PEER_BLOCK, PEER_HEADS, PEER_TOPK * PEER_TOPK), sel, axis=-1)
        g = jax.nn.softmax(best.astype(jnp.float32), axis=-1).astype(xb.dtype)
        act = jax.nn.gelu(jnp.einsum('thkd,td->thk', expert_u[experts], xb))
        return jnp.einsum('thk,thkd->td', g * act, expert_v[experts])

    return lax.map(block, blocks).reshape(bsz, l, d)


def setup_inputs(seed: int = 0) -> dict:
    key = jax.random.key(seed)
    ks = jax.random.split(key, 20)
    f32 = jnp.float32
    d = D_MODEL
    nrm = lambda k, shape, s: jax.random.normal(k, shape, f32) * s
    base_decay = jnp.log(2.0 ** (-5.0 - jnp.arange(RET_HEADS, dtype=f32)))
    return {
        "x": nrm(ks[0], (BATCH, SEQ, d), 1.0),
        "c": nrm(ks[1], (BATCH, d), 1.0),
        "ctx": nrm(ks[2], (BATCH, CTX_LEN, d), 1.0),
        "c_ctx": nrm(ks[3], (d,), 1.0),
        "w_mod": nrm(ks[4], (DEPTH, d, N_MOD * d), 0.5 * d ** -0.5),
        "b_mod": nrm(ks[5], (DEPTH, N_MOD * d), 0.02),
        "norm_gain": 1.0 + nrm(ks[6], (DEPTH, 4, d), 0.05),
        "w_in": nrm(ks[7], (DEPTH, d, IN_WIDTH), d ** -0.5),
        "ret_decay_raw": jnp.broadcast_to(base_decay, (DEPTH, 2, RET_HEADS)) + nrm(ks[8], (DEPTH, 2, RET_HEADS), 0.05),
        "w_ret_o": nrm(ks[9], (DEPTH, RET_V_W, d), RET_V_W ** -0.5),
        "conv_w": nrm(ks[10], (DEPTH, CONV_WIDTH, CONV_CH), CONV_WIDTH ** -0.5),
        "conv_b": nrm(ks[11], (DEPTH, CONV_CH), 0.02),
        "conv_norm_g": 1.0 + nrm(ks[12], (DEPTH, CONV_CH), 0.05),
        "conv_norm_b": nrm(ks[13], (DEPTH, CONV_CH), 0.02),
        "w_conv_o": nrm(ks[14], (DEPTH, CONV_CH, d), CONV_CH ** -0.5),
        "w_out": nrm(ks[15], (DEPTH, d, d), d ** -0.5),
        "peer_wq": nrm(ks[16], (DEPTH, d, PEER_HEADS * PEER_DK), d ** -0.5),
        "peer_keys": nrm(ks[17], (DEPTH, PEER_HEADS, 2, PEER_N_KEYS, PEER_DK_HALF), PEER_DK_HALF ** -0.5),
        "peer_u": nrm(ks[18], (DEPTH, PEER_N_EXPERTS, d), d ** -0.5),
        "peer_v": nrm(ks[19], (DEPTH, PEER_N_EXPERTS, d), (PEER_HEADS * PEER_TOPK) ** -0.5),
    }


def reference(x, c, ctx, c_ctx, w_mod, b_mod, norm_gain, w_in, ret_decay_raw, w_ret_o, conv_w, conv_b,
              conv_norm_g, conv_norm_b, w_conv_o, w_out, peer_wq, peer_keys, peer_u, peer_v):
    lat, cx = x, ctx
    n_lat = lat.shape[1]
    n_rows = n_lat // GRID_W
    rope = _axial_rotary(n_rows)
    for layer in range(DEPTH):
        last = layer == DEPTH - 1
        gain = norm_gain[layer]
        mod_lat = (jax.nn.silu(c) @ w_mod[layer] + b_mod[layer])[:, None, :]
        mod_ctx = (jax.nn.silu(c_ctx) @ w_mod[layer] + b_mod[layer])[None, None, :]
        sh1, sc1, g1, sh2, sc2, g2 = jnp.split(mod_lat, N_MOD, axis=-1)
        csh1, csc1, cg1, csh2, csc2, cg2 = jnp.split(mod_ctx, N_MOD, axis=-1)
        log_gamma = -jax.nn.softplus(ret_decay_raw[layer])
        mixer_w = (w_in[layer], log_gamma, w_ret_o[layer], conv_w[layer], conv_b[layer],
                   conv_norm_g[layer], conv_norm_b[layer], w_conv_o[layer], w_out[layer])
        peer_w = (peer_wq[layer], peer_keys[layer], peer_u[layer], peer_v[layer])
        hc = _modulate(_rms_norm(cx, gain[0]), csh1, csc1)
        st_f, st_b = _context_states(hc, w_in[layer], log_gamma)
        h = _modulate(_rms_norm(lat, gain[0]), sh1, sc1)
        y = _mixer(h, *mixer_w, rope, st_f, st_b)
        lat = lat + g1 * _rms_norm(y, gain[1])
        h = _modulate(_rms_norm(lat, gain[2]), sh2, sc2)
        lat = lat + g2 * _rms_norm(_peer(h, *peer_w), gain[3])
        if not last:
            zeros = jnp.zeros((cx.shape[0], RET_HEADS, RET_DK, RET_DV), jnp.float32)
            yc = _mixer(hc, *mixer_w, None, zeros, zeros)
            cx = cx + cg1 * _rms_norm(yc, gain[1])
            hc2 = _modulate(_rms_norm(cx, gain[2]), csh2, csc2)
            cx = cx + cg2 * _rms_norm(_peer(hc2, *peer_w), gain[3])
    return lat
```

```python
import functools

import jax
import jax.numpy as jnp
from jax import lax
from jax.experimental import pallas as pl
from jax.experimental.pallas import tpu as pltpu
from jax.experimental.pallas import tpu_sc as plsc

F32 = jnp.float32
BF16 = jnp.bfloat16

D_MODEL = 1024
GRID_W = 64
EPS = 1e-6
N_MOD = 6
RET_HEADS = 4
RET_DK = 256
RET_CHUNK = 128
ROPE_BASE = 10000.0
CONV_WIDTH = 31
CONV_HALF = CONV_WIDTH // 2
PEER_HEADS = 8
PEER_N_KEYS = 128
PEER_DK_HALF = 128
PEER_TOPK = 16
PEER_SEL = PEER_HEADS * PEER_TOPK

SC_CORES = 2
SC_SUBCORES = 16
SC_WORKERS = SC_CORES * SC_SUBCORES
SC_LANES = 16
SC_ROWS = 32

VMEM_LIMIT = 48 * 1024 * 1024


def _cparams(*sem):
    return pltpu.CompilerParams(dimension_semantics=sem, vmem_limit_bytes=VMEM_LIMIT)


def _sigmoid(x):
    return 1.0 / (1.0 + jnp.exp(-x))


def _silu(x):
    return x * _sigmoid(x)


def _softplus(x):
    return jnp.maximum(x, 0.0) + jnp.log1p(jnp.exp(-jnp.abs(x)))


def _rms(x, gain):
    return x * lax.rsqrt(jnp.mean(x * x, axis=-1, keepdims=True) + EPS) * gain


def _mod_kernel(c_ref, w_ref, b_ref, o_ref):
    a = _silu(c_ref[...])
    o_ref[...] = jnp.dot(a, w_ref[...], preferred_element_type=F32,
                         precision=lax.Precision.HIGHEST) + b_ref[...]


def _modulation(c_all, w_mod, b_mod):
    rows, d = c_all.shape
    n = w_mod.shape[1]
    return pl.pallas_call(
        _mod_kernel,
        grid=(n // d,),
        in_specs=[pl.BlockSpec((rows, d), lambda j: (0, 0)),
                  pl.BlockSpec((d, d), lambda j: (0, j)),
                  pl.BlockSpec((1, d), lambda j: (0, j))],
        out_specs=pl.BlockSpec((rows, d), lambda j: (0, j)),
        out_shape=jax.ShapeDtypeStruct((rows, n), F32),
        compiler_params=_cparams("arbitrary"),
        name="modulation",
    )(c_all, w_mod, b_mod)


def _ctx_kernel(raw_ref, ctx_ref, gain_ref, sh_ref, sc_ref, wkv_ref, sf_ref, sb_ref):
    x = ctx_ref[0]
    lc = x.shape[0]
    hc = _rms(x, gain_ref[...]) * (1.0 + sc_ref[...]) + sh_ref[...]
    kv = jnp.dot(hc.astype(BF16), wkv_ref[...], preferred_element_type=F32)
    pos = lax.broadcasted_iota(jnp.int32, (lc, RET_DK), 0).astype(F32)
    tn = (((0,), (0,)), ((), ()))
    for h in range(RET_HEADS):
        lgf = -_softplus(jnp.full((lc, RET_DK), raw_ref[0, h], F32))
        lgb = -_softplus(jnp.full((lc, RET_DK), raw_ref[1, h], F32))
        k = kv[:, h * RET_DK:(h + 1) * RET_DK] * (RET_DK ** -0.5)
        v = kv[:, (RET_HEADS + h) * RET_DK:(RET_HEADS + h + 1) * RET_DK].astype(BF16)
        kf = (k * jnp.exp(lgf * (lc - 1.0 - pos))).astype(BF16)
        kb = (k * jnp.exp(lgb * pos)).astype(BF16)
        sf_ref[0, h] = lax.dot_general(kf, v, tn, preferred_element_type=F32)
        sb_ref[0, h] = lax.dot_general(kb, v, tn, preferred_element_type=F32)


def _context_states(raw, ctx, gain0, csh, csc, w_kv):
    b, lc, d = ctx.shape
    st = jax.ShapeDtypeStruct((b, RET_HEADS, RET_DK, RET_DK), F32)
    st_spec = pl.BlockSpec((1, RET_HEADS, RET_DK, RET_DK), lambda i: (i, 0, 0, 0))
    vec = pl.BlockSpec((1, d), lambda i: (0, 0))
    return pl.pallas_call(
        _ctx_kernel,
        grid=(b,),
        in_specs=[pl.BlockSpec(memory_space=pltpu.SMEM),
                  pl.BlockSpec((1, lc, d), lambda i: (i, 0, 0)),
                  vec, vec, vec,
                  pl.BlockSpec(w_kv.shape, lambda i: (0, 0))],
        out_specs=[st_spec, st_spec],
        out_shape=[st, st],
        compiler_params=_cparams("arbitrary"),
        name="context_states",
    )(raw, ctx, gain0, csh, csc, w_kv)


IN_TM = 1024
IN_TILES = 8
IN_OUT_TILES = 7


def _inproj_kernel(x_ref, gain_ref, sh_ref, sc_ref, cos_ref, sin_ref, w_ref, o_ref, xn_ref):
    j = pl.program_id(1)

    @pl.when(j == 0)
    def _():
        h = _rms(x_ref[...], gain_ref[...]) * (1.0 + sc_ref[0]) + sh_ref[0]
        xn_ref[...] = h.astype(BF16)

    acc = jnp.dot(xn_ref[...], w_ref[...], preferred_element_type=F32)
    half = RET_DK // 2

    def rotary(scale):
        cos = cos_ref[...]
        sin = sin_ref[...]
        for h in range(RET_HEADS):
            t1 = acc[:, h * RET_DK:h * RET_DK + half]
            t2 = acc[:, h * RET_DK + half:(h + 1) * RET_DK]
            o_ref[:, h * RET_DK:h * RET_DK + half] = ((t1 * cos - t2 * sin) * scale).astype(BF16)
            o_ref[:, h * RET_DK + half:(h + 1) * RET_DK] = ((t1 * sin + t2 * cos) * scale).astype(BF16)

    @pl.when(j == 0)
    def _():
        rotary(1.0)

    @pl.when(j == 1)
    def _():
        rotary(RET_DK ** -0.5)

    @pl.when(j == 2)
    def _():
        o_ref[...] = acc.astype(BF16)

    @pl.when(j == 3)
    def _():
        o_ref[...] = _silu(acc).astype(BF16)

    hw = D_MODEL // 2

    @pl.when(j == 4)
    def _():
        o_ref[:, :hw] = (acc[:, :hw] * _sigmoid(acc[:, hw:])).astype(BF16)

    @pl.when(j == 5)
    def _():
        o_ref[:, hw:] = (acc[:, :hw] * _sigmoid(acc[:, hw:])).astype(BF16)

    @pl.when(j >= 6)
    def _():
        o_ref[...] = _sigmoid(acc).astype(BF16)


def _out_tile(j):
    return jnp.where(j <= 4, j, j - 1)


def _in_projection(x2, gain0, sh1, sc1, cos, sin, w_perm, seq):
    n, d = x2.shape
    tiles_per_seq = seq // IN_TM
    return pl.pallas_call(
        _inproj_kernel,
        grid=(n // IN_TM, IN_TILES),
        in_specs=[pl.BlockSpec((IN_TM, d), lambda i, j: (i, 0)),
                  pl.BlockSpec((1, d), lambda i, j: (0, 0)),
                  pl.BlockSpec((1, 1, d), lambda i, j: (i // tiles_per_seq, 0, 0)),
                  pl.BlockSpec((1, 1, d), lambda i, j: (i // tiles_per_seq, 0, 0)),
                  pl.BlockSpec((IN_TM, RET_DK // 2), lambda i, j: (i % tiles_per_seq, 0)),
                  pl.BlockSpec((IN_TM, RET_DK // 2), lambda i, j: (i % tiles_per_seq, 0)),
                  pl.BlockSpec((d, d), lambda i, j: (0, j))],
        out_specs=pl.BlockSpec((IN_TM, d), lambda i, j: (i, _out_tile(j))),
        out_shape=jax.ShapeDtypeStruct((n, IN_OUT_TILES * d), BF16),
        scratch_shapes=[pltpu.VMEM((IN_TM, d), BF16)],
        compiler_params=_cparams("arbitrary", "arbitrary"),
        name="in_projection",
    )(x2, gain0, sh1, sc1, cos, sin, w_perm)


def _ret_kernel(raw_ref, s0f_ref, s0b_ref, q_ref, k_ref, v_ref, g_ref, o_ref, sf_ref, sb_ref, y_ref):
    hd = pl.program_id(1)
    c = RET_CHUNK
    seq = q_ref.shape[0]
    nc = seq // c
    def log_gamma(direction, shape):
        return -_softplus(jnp.full(shape, raw_ref[direction, hd], F32))

    ri = lax.broadcasted_iota(jnp.int32, (c, c), 0).astype(F32)
    ci = lax.broadcasted_iota(jnp.int32, (c, c), 1).astype(F32)
    dec_f = jnp.where(ri >= ci, jnp.exp(log_gamma(0, (c, c)) * jnp.maximum(ri - ci, 0.0)), 0.0)
    dec_b = jnp.where(ci >= ri, jnp.exp(log_gamma(1, (c, c)) * jnp.maximum(ci - ri, 0.0)), 0.0)
    pos = lax.broadcasted_iota(jnp.int32, (c, RET_DK), 0).astype(F32)
    lgf = log_gamma(0, (c, RET_DK))
    lgb = log_gamma(1, (c, RET_DK))
    qdec_f = jnp.exp(lgf * (pos + 1.0))
    kdec_f = jnp.exp(lgf * (c - 1.0 - pos))
    qdec_b = jnp.exp(lgb * (c - pos))
    kdec_b = jnp.exp(lgb * pos)
    cdec_f = jnp.exp(log_gamma(0, (1, RET_DK)) * c)
    cdec_b = jnp.exp(log_gamma(1, (1, RET_DK)) * c)
    sf_ref[...] = s0f_ref[0, 0]
    sb_ref[...] = s0b_ref[0, 0]
    nt = (((1,), (1,)), ((), ()))
    tn = (((0,), (0,)), ((), ()))

    def chunk(st_ref, row, dec, qdec, kdec, cdec):
        q = q_ref[pl.ds(row, c), :]
        k = k_ref[pl.ds(row, c), :]
        v = v_ref[pl.ds(row, c), :]
        s = lax.dot_general(q, k, nt, preferred_element_type=F32) * dec
        st = st_ref[...]
        y = jnp.dot(s.astype(BF16), v, preferred_element_type=F32)
        y = y + jnp.dot((q.astype(F32) * qdec).astype(BF16), st.astype(BF16), preferred_element_type=F32)
        kd = (k.astype(F32) * kdec).astype(BF16)
        st_ref[...] = st * cdec + lax.dot_general(kd, v, tn, preferred_element_type=F32)
        return y

    def first_half(i, carry):
        rf = pl.multiple_of(i * c, c)
        rb = pl.multiple_of((nc - 1 - i) * c, c)
        y_ref[pl.ds(rf, c), :] = chunk(sf_ref, rf, dec_f, qdec_f, kdec_f, cdec_f)
        y_ref[pl.ds(rb, c), :] = chunk(sb_ref, rb, dec_b, qdec_b, kdec_b, cdec_b)
        return carry

    def second_half(i, carry):
        rf = pl.multiple_of(i * c, c)
        rb = pl.multiple_of((nc - 1 - i) * c, c)
        y_ref[pl.ds(rf, c), :] += chunk(sf_ref, rf, dec_f, qdec_f, kdec_f, cdec_f)
        y_ref[pl.ds(rb, c), :] += chunk(sb_ref, rb, dec_b, qdec_b, kdec_b, cdec_b)
        return carry

    lax.fori_loop(0, nc // 2, first_half, 0)
    lax.fori_loop(nc // 2, nc, second_half, 0)

    def finish(i, carry):
        r = pl.multiple_of(i * c, c)
        y = y_ref[pl.ds(r, c), :]
        yn = y * lax.rsqrt(jnp.mean(y * y, axis=-1, keepdims=True) + EPS)
        o_ref[pl.ds(r, c), :] = (yn * g_ref[pl.ds(r, c), :].astype(F32)).astype(BF16)
        return carry

    lax.fori_loop(0, nc, finish, 0)


def _retention(raw, st_f, st_b, proj, batch, seq):
    n = batch * seq
    st_spec = pl.BlockSpec((1, 1, RET_DK, RET_DK), lambda b, h: (b, h, 0, 0))

    def col(tile):
        return pl.BlockSpec((seq, RET_DK), lambda b, h: (b, tile * RET_HEADS + h))

    return pl.pallas_call(
        _ret_kernel,
        grid=(batch, RET_HEADS),
        in_specs=[pl.BlockSpec(memory_space=pltpu.SMEM), st_spec, st_spec,
                  col(0), col(1), col(2), col(3)],
        out_specs=pl.BlockSpec((seq, RET_DK), lambda b, h: (b, h)),
        out_shape=jax.ShapeDtypeStruct((n, RET_HEADS * RET_DK), BF16),
        scratch_shapes=[pltpu.VMEM((RET_DK, RET_DK), F32), pltpu.VMEM((RET_DK, RET_DK), F32),
                        pltpu.VMEM((seq, RET_DK), F32)],
        compiler_params=_cparams("arbitrary", "arbitrary"),
        name="retention",
    )(raw, st_f, st_b, proj, proj, proj, proj)


CONV_TL = 256
CONV_HALO = 16
CONV_RB = 32


def _conv_kernel(prev_ref, cur_ref, next_ref, cw_ref, cb_ref, lg_ref, lb_ref, o_ref, pad_ref, *, seq):
    i = pl.program_id(0)
    tl = CONV_TL
    has_prev = (i * tl) % seq != 0
    has_next = ((i + 1) * tl) % seq != 0
    pad_ref[0:CONV_HALO, :] = jnp.where(has_prev, prev_ref[...].astype(F32), 0.0)
    pad_ref[CONV_HALO:CONV_HALO + tl, :] = cur_ref[...].astype(F32)
    pad_ref[CONV_HALO + tl:, :] = jnp.where(has_next, next_ref[...].astype(F32), 0.0)
    off = CONV_HALO - CONV_HALF
    for rb in range(tl // CONV_RB):
        base = rb * CONV_RB + off
        acc = pad_ref[base:base + CONV_RB, :] * cw_ref[0:1, :]
        for w in range(1, CONV_WIDTH):
            acc = acc + pad_ref[base + w:base + w + CONV_RB, :] * cw_ref[w:w + 1, :]
        u = acc + cb_ref[...]
        mu = jnp.mean(u, axis=-1, keepdims=True)
        uc = u - mu
        y = uc * lax.rsqrt(jnp.mean(uc * uc, axis=-1, keepdims=True) + EPS) * lg_ref[...] + lb_ref[...]
        o_ref[rb * CONV_RB:(rb + 1) * CONV_RB, :] = _silu(y).astype(BF16)


def _conv_branch(proj, conv_w, conv_b, ln_g, ln_b, seq):
    n = proj.shape[0]
    d = D_MODEL
    glu_tile = 4
    hb = CONV_TL // CONV_HALO
    n_halo = n // CONV_HALO
    vec = pl.BlockSpec((1, d), lambda i: (0, 0))
    return pl.pallas_call(
        functools.partial(_conv_kernel, seq=seq),
        grid=(n // CONV_TL,),
        in_specs=[pl.BlockSpec((CONV_HALO, d), lambda i: (jnp.maximum(i * hb - 1, 0), glu_tile)),
                  pl.BlockSpec((CONV_TL, d), lambda i: (i, glu_tile)),
                  pl.BlockSpec((CONV_HALO, d), lambda i: (jnp.minimum((i + 1) * hb, n_halo - 1), glu_tile)),
                  pl.BlockSpec((CONV_WIDTH, d), lambda i: (0, 0)),
                  vec, vec, vec],
        out_specs=pl.BlockSpec((CONV_TL, d), lambda i: (i, 0)),
        out_shape=jax.ShapeDtypeStruct((n, d), BF16),
        scratch_shapes=[pltpu.VMEM((CONV_TL + 2 * CONV_HALO, d), F32)],
        compiler_params=_cparams("arbitrary"),
        name="conv_branch",
    )(proj, proj, proj, conv_w, conv_b, ln_g, ln_b)


MERGE_TM = 256


def _merge_kernel(x_ref, yr_ref, uc_ref, ga_ref, gb_ref, wr_ref, wc_ref, wo_ref, wq_ref,
                  gain1_ref, gain2_ref, g1_ref, sh2_ref, sc2_ref, lat_ref, h2_ref, qp_ref):
    a = jnp.dot(yr_ref[...], wr_ref[...], preferred_element_type=F32)
    b = jnp.dot(uc_ref[...], wc_ref[...], preferred_element_type=F32)
    merged = ga_ref[...].astype(F32) * a + gb_ref[...].astype(F32) * b
    y = jnp.dot(merged.astype(BF16), wo_ref[...], preferred_element_type=F32)
    lat = x_ref[...] + g1_ref[0] * _rms(y, gain1_ref[...])
    lat_ref[...] = lat
    h2 = _rms(lat, gain2_ref[...]) * (1.0 + sc2_ref[0]) + sh2_ref[0]
    h2_ref[...] = h2
    qp_ref[...] = jnp.dot(h2.astype(BF16), wq_ref[...], preferred_element_type=F32).astype(BF16)


def _merge(x2, y_ret, u_conv, proj, w_ret_o, w_conv_o, w_out, w_q, gain1, gain2, g1, sh2, sc2, seq):
    n, d = x2.shape
    tps = seq // MERGE_TM
    nq = w_q.shape[1]
    row = pl.BlockSpec((MERGE_TM, d), lambda i: (i, 0))
    wsp = pl.BlockSpec((d, d), lambda i: (0, 0))
    vec = pl.BlockSpec((1, d), lambda i: (0, 0))
    bvec = pl.BlockSpec((1, 1, d), lambda i: (i // tps, 0, 0))
    return pl.pallas_call(
        _merge_kernel,
        grid=(n // MERGE_TM,),
        in_specs=[row, row, row,
                  pl.BlockSpec((MERGE_TM, d), lambda i: (i, 5)),
                  pl.BlockSpec((MERGE_TM, d), lambda i: (i, 6)),
                  wsp, wsp, wsp,
                  pl.BlockSpec((d, nq), lambda i: (0, 0)),
                  vec, vec, bvec, bvec, bvec],
        out_specs=[row, row, pl.BlockSpec((MERGE_TM, nq), lambda i: (i, 0))],
        out_shape=[jax.ShapeDtypeStruct((n, d), F32), jax.ShapeDtypeStruct((n, d), F32),
                   jax.ShapeDtypeStruct((n, nq), BF16)],
        compiler_params=_cparams("arbitrary"),
        name="merge",
    )(x2, y_ret, u_conv, proj, proj, w_ret_o, w_conv_o, w_out, w_q, gain1, gain2, g1, sh2, sc2)


ROUTE_T = 512
LANE = 128


def _topk_rows(s, k, payload=None):
    r = s.shape[0]
    rows = lax.broadcasted_iota(jnp.int32, s.shape, 0).astype(F32)
    vals, idxs = [], []
    for _ in range(k):
        m = jnp.max(s, axis=0, keepdims=True)
        pos = jnp.min(jnp.where(s == m, rows, float(r)), axis=0, keepdims=True)
        hit = rows == pos
        vals.append(m)
        if payload is None:
            idxs.append(pos)
        else:
            idxs.append(jnp.max(jnp.where(hit, payload, -1.0), axis=0, keepdims=True))
        s = jnp.where(hit, -jnp.inf, s)
    return jnp.concatenate(vals, axis=0), jnp.concatenate(idxs, axis=0)


def _route_kernel(q_ref, keys_ref, e_ref, g_ref, es_ref, gs_ref):
    h = pl.program_id(1)
    nt = (((1,), (1,)), ((), ()))
    row0 = pl.multiple_of(h * PEER_TOPK, PEER_TOPK)
    for t in range(ROUTE_T // LANE):
        cols = slice(t * LANE, (t + 1) * LANE)
        tops = []
        for p in range(2):
            qh = q_ref[cols, p * PEER_DK_HALF:(p + 1) * PEER_DK_HALF]
            s = lax.dot_general(keys_ref[0, p], qh, nt, preferred_element_type=F32)
            tops.append(_topk_rows(s, PEER_TOPK))
        (v1, i1), (v2, i2) = tops
        cand = jnp.concatenate([v1[a:a + 1] + v2 for a in range(PEER_TOPK)], axis=0)
        cidx = jnp.concatenate([i1[a:a + 1] * float(PEER_N_KEYS) + i2 for a in range(PEER_TOPK)], axis=0)
        best, experts = _topk_rows(cand, PEER_TOPK, payload=cidx)
        ex = jnp.exp(best - best[0:1])
        gates = ex / jnp.sum(ex, axis=0, keepdims=True)
        es_ref[pl.ds(row0, PEER_TOPK), cols] = experts.astype(jnp.int32)
        gs_ref[pl.ds(row0, PEER_TOPK), cols] = gates

    @pl.when(h == PEER_HEADS - 1)
    def _():
        e_ref[...] = es_ref[...].T
        g_ref[...] = gs_ref[...].T


def _route(qp, keys):
    n = qp.shape[0]
    hw = 2 * PEER_DK_HALF
    out = pl.BlockSpec((ROUTE_T, PEER_SEL), lambda i, h: (i, 0))
    return pl.pallas_call(
        _route_kernel,
        grid=(n // ROUTE_T, PEER_HEADS),
        in_specs=[pl.BlockSpec((ROUTE_T, hw), lambda i, h: (i, h)),
                  pl.BlockSpec((1, 2, PEER_N_KEYS, PEER_DK_HALF), lambda i, h: (h, 0, 0, 0))],
        out_specs=[out, out],
        out_shape=[jax.ShapeDtypeStruct((n, PEER_SEL), jnp.int32),
                   jax.ShapeDtypeStruct((n, PEER_SEL), F32)],
        scratch_shapes=[pltpu.VMEM((PEER_SEL, ROUTE_T), jnp.int32), pltpu.VMEM((PEER_SEL, ROUTE_T), F32)],
        compiler_params=_cparams("arbitrary", "arbitrary"),
        name="peer_route",
    )(qp, keys)


def _sc_mesh():
    return plsc.VectorSubcoreMesh(core_axis_name="c", subcore_axis_name="s")


def _worker_id():
    return lax.axis_index("s") * SC_CORES + lax.axis_index("c")


def _peer_scores_sc(table, h2, experts):
    n, d = h2.shape
    per = n // SC_WORKERS
    nchunk = PEER_SEL // SC_ROWS
    ncol = d // SC_LANES

    @functools.partial(
        pl.kernel, mesh=_sc_mesh(),
        out_type=jax.ShapeDtypeStruct((n, PEER_SEL), F32),
        scratch_types=[pltpu.VMEM((PEER_SEL,), jnp.int32), pltpu.VMEM((d,), F32),
                       pltpu.VMEM((SC_ROWS, d), F32), pltpu.VMEM((SC_ROWS, d), F32),
                       pltpu.VMEM((PEER_SEL,), F32),
                       pltpu.SemaphoreType.DMA, pltpu.SemaphoreType.DMA],
        compiler_params=pltpu.CompilerParams(needs_layout_passes=False),
        name="peer_scores_sc",
    )
    def run(tab_hbm, x_hbm, idx_hbm, out_hbm, idx_v, x_v, rows0, rows1, res_v, sem0, sem1):
        base = _worker_id() * per
        bufs = (rows0, rows1)
        sems = (sem0, sem1)
        lane = lax.iota(jnp.int32, SC_LANES)

        def gather(ch):
            return pltpu.make_async_copy(tab_hbm.at[idx_v.at[pl.ds(ch * SC_ROWS, SC_ROWS)]],
                                         bufs[ch % 2], sems[ch % 2])

        @pl.loop(0, per)
        def _(t):
            tok = base + t
            pltpu.sync_copy(idx_hbm.at[tok], idx_v)
            pltpu.sync_copy(x_hbm.at[tok], x_v)
            gather(0).start()
            for ch in range(nchunk):
                if ch + 1 < nchunk:
                    gather(ch + 1).start()
                gather(ch).wait()
                rows = bufs[ch % 2]

                def col_step(cidx, accs):
                    off = pl.multiple_of(cidx * SC_LANES, SC_LANES)
                    xc = x_v[pl.ds(off, SC_LANES)]
                    return tuple(accs[r] + rows[r, pl.ds(off, SC_LANES)] * xc for r in range(SC_ROWS))

                zero = jnp.zeros((SC_LANES,), F32)
                accs = lax.fori_loop(0, ncol, col_step, (zero,) * SC_ROWS)
                for g in range(SC_ROWS // SC_LANES):
                    res = zero
                    for r in range(SC_LANES):
                        tot = jnp.sum(accs[g * SC_LANES + r])
                        res = jnp.where(lane == r, tot, res)
                    res_v[pl.ds(ch * SC_ROWS + g * SC_LANES, SC_LANES)] = res
            pltpu.sync_copy(res_v, out_hbm.at[tok])

    return run(table, h2, experts)


def _peer_combine_sc(table, weights, experts):
    n = weights.shape[0]
    d = table.shape[1]
    per = n // SC_WORKERS
    nchunk = PEER_SEL // SC_ROWS
    half_cols = d // SC_LANES // 2

    @functools.partial(
        pl.kernel, mesh=_sc_mesh(),
        out_type=jax.ShapeDtypeStruct((n, d), F32),
        scratch_types=[pltpu.VMEM((PEER_SEL,), jnp.int32), pltpu.VMEM((PEER_SEL,), F32),
                       pltpu.VMEM((SC_ROWS, d), F32), pltpu.VMEM((SC_ROWS, d), F32),
                       pltpu.VMEM((d,), F32),
                       pltpu.SemaphoreType.DMA, pltpu.SemaphoreType.DMA],
        compiler_params=pltpu.CompilerParams(needs_layout_passes=False),
        name="peer_combine_sc",
    )
    def run(tab_hbm, w_hbm, idx_hbm, out_hbm, idx_v, w_v, rows0, rows1, acc_v, sem0, sem1):
        base = _worker_id() * per
        bufs = (rows0, rows1)
        sems = (sem0, sem1)

        def gather(ch):
            return pltpu.make_async_copy(tab_hbm.at[idx_v.at[pl.ds(ch * SC_ROWS, SC_ROWS)]],
                                         bufs[ch % 2], sems[ch % 2])

        @pl.loop(0, per)
        def _(t):
            tok = base + t
            pltpu.sync_copy(idx_hbm.at[tok], idx_v)
            pltpu.sync_copy(w_hbm.at[tok], w_v)
            gather(0).start()
            for ch in range(nchunk):
                if ch + 1 < nchunk:
                    gather(ch + 1).start()
                gather(ch).wait()
                rows = bufs[ch % 2]
                for hf in range(2):
                    c0 = hf * half_cols * SC_LANES
                    if ch == 0:
                        init = (jnp.zeros((SC_LANES,), F32),) * half_cols
                    else:
                        init = tuple(acc_v[pl.ds(c0 + cc * SC_LANES, SC_LANES)] for cc in range(half_cols))

                    def row_step(r, accs):
                        wr = plsc.load_gather(w_v, [jnp.full((SC_LANES,), ch * SC_ROWS, jnp.int32) + r])
                        return tuple(accs[cc] + wr * rows[r, pl.ds(c0 + cc * SC_LANES, SC_LANES)]
                                     for cc in range(half_cols))

                    accs = lax.fori_loop(0, SC_ROWS, row_step, init)
                    for cc in range(half_cols):
                        acc_v[pl.ds(c0 + cc * SC_LANES, SC_LANES)] = accs[cc]
            pltpu.sync_copy(acc_v, out_hbm.at[tok])

    return run(table, weights, experts)


EW_TM = 2048


def _gelu_tanh(x):
    c = 0.7978845608028654
    return 0.5 * x * (1.0 + jnp.tanh(c * (x + 0.044715 * (x * x * x))))


def _gate_kernel(s_ref, g_ref, o_ref):
    o_ref[...] = g_ref[...] * _gelu_tanh(s_ref[...])


def _gate_weights(scores, gates):
    n, k = scores.shape
    spec = pl.BlockSpec((EW_TM, k), lambda i: (i, 0))
    return pl.pallas_call(
        _gate_kernel, grid=(n // EW_TM,), in_specs=[spec, spec], out_specs=spec,
        out_shape=jax.ShapeDtypeStruct((n, k), F32),
        compiler_params=_cparams("arbitrary"), name="peer_gate",
    )(scores, gates)


FIN_TM = 1024


def _final_kernel(lat_ref, p_ref, gain_ref, g2_ref, o_ref):
    o_ref[...] = lat_ref[...] + g2_ref[0] * _rms(p_ref[...], gain_ref[...])


def _final(lat, peer, gain3, g2, seq):
    n, d = lat.shape
    tps = seq // FIN_TM
    row = pl.BlockSpec((FIN_TM, d), lambda i: (i, 0))
    return pl.pallas_call(
        _final_kernel, grid=(n // FIN_TM,),
        in_specs=[row, row, pl.BlockSpec((1, d), lambda i: (0, 0)),
                  pl.BlockSpec((1, 1, d), lambda i: (i // tps, 0, 0))],
        out_specs=row, out_shape=jax.ShapeDtypeStruct((n, d), F32),
        compiler_params=_cparams("arbitrary"), name="final_residual",
    )(lat, peer, gain3, g2)


def _axial_rotary(seq):
    n_rows = seq // GRID_W
    rows = jnp.repeat(jnp.arange(n_rows, dtype=F32), GRID_W)
    cols = jnp.tile(jnp.arange(GRID_W, dtype=F32), n_rows)
    quarter = RET_DK // 4
    inv = ROPE_BASE ** (-jnp.arange(quarter, dtype=F32) / quarter)
    ang = jnp.concatenate([rows[:, None] * inv, cols[:, None] * inv], axis=-1)
    return jnp.cos(ang), jnp.sin(ang)


def kernel(x, c, ctx, c_ctx, w_mod, b_mod, norm_gain, w_in, ret_decay_raw, w_ret_o, conv_w, conv_b,
           conv_norm_g, conv_norm_b, w_conv_o, w_out, peer_wq, peer_keys, peer_u, peer_v):
    assert w_mod.shape[0] == 1, "single layer"
    batch, seq, d = x.shape
    n = batch * seq
    assert d == D_MODEL and seq % IN_TM == 0 and n % (SC_WORKERS * 8) == 0
    gain = norm_gain[0]
    x2 = x.reshape(n, d)

    pad = (-(batch + 1)) % 8
    c_all = jnp.concatenate([c, c_ctx[None, :], jnp.zeros((pad, d), F32)], axis=0)
    mod = _modulation(c_all, w_mod[0], b_mod[0][None, :])
    mod_lat = mod[:batch].reshape(batch, N_MOD, 1, d)
    sh1, sc1, g1, sh2, sc2, g2 = (mod_lat[:, i] for i in range(N_MOD))
    mod_ctx = mod[batch:batch + 1].reshape(N_MOD, 1, d)

    w_in0 = w_in[0]
    q_w = RET_HEADS * RET_DK
    w_kv = w_in0[:, q_w:3 * q_w].astype(BF16)
    st_f, st_b = _context_states(ret_decay_raw[0], ctx, gain[0:1], mod_ctx[0], mod_ctx[1], w_kv)

    hw = d // 2
    glu_a = w_in0[:, 4 * d:5 * d]
    glu_b = w_in0[:, 5 * d:6 * d]
    w_perm = jnp.concatenate([w_in0[:, :4 * d], glu_a[:, :hw], glu_b[:, :hw], glu_a[:, hw:], glu_b[:, hw:],
                              w_in0[:, 6 * d:]], axis=1).astype(BF16)
    cos, sin = _axial_rotary(seq)
    proj = _in_projection(x2, gain[0:1], sh1, sc1, cos, sin, w_perm, seq)

    y_ret = _retention(ret_decay_raw[0], st_f, st_b, proj, batch, seq)
    u_conv = _conv_branch(proj, conv_w[0], conv_b[0][None, :], conv_norm_g[0][None, :],
                          conv_norm_b[0][None, :], seq)
    lat1, h2, qp = _merge(x2, y_ret, u_conv, proj, w_ret_o[0].astype(BF16), w_conv_o[0].astype(BF16),
                          w_out[0].astype(BF16), peer_wq[0].astype(BF16), gain[1:2], gain[2:3],
                          g1, sh2, sc2, seq)

    experts, gates = _route(qp, peer_keys[0].astype(BF16))
    scores = _peer_scores_sc(peer_u[0], h2, experts)
    weights = _gate_weights(scores, gates)
    peer_out = _peer_combine_sc(peer_v[0], weights, experts)
    out = _final(lat1, peer_out, gain[3:4], g2, seq)
    return out.reshape(batch, seq, d)
```

```python
import functools

import jax
import jax.numpy as jnp
from jax import lax
from jax.experimental import pallas as pl
from jax.experimental.pallas import tpu as pltpu
from jax.experimental.pallas import tpu_sc as plsc

F32 = jnp.float32
BF16 = jnp.bfloat16

D_MODEL = 1024
GRID_W = 64
EPS = 1e-6
N_MOD = 6
RET_HEADS = 4
RET_DK = 256
RET_CHUNK = 128
ROPE_BASE = 10000.0
CONV_WIDTH = 31
CONV_HALF = CONV_WIDTH // 2
PEER_HEADS = 8
PEER_N_KEYS = 128
PEER_DK_HALF = 128
PEER_TOPK = 16
PEER_SEL = PEER_HEADS * PEER_TOPK

SC_CORES = 2
SC_SUBCORES = 16
SC_WORKERS = SC_CORES * SC_SUBCORES
SC_LANES = 16
SC_ROWS = 32
SC_TB = 16
SUBLANE = 8

VMEM_LIMIT = 48 * 1024 * 1024


def _cparams(*sem):
    return pltpu.CompilerParams(dimension_semantics=sem, vmem_limit_bytes=VMEM_LIMIT)


def _sigmoid(x):
    return 1.0 / (1.0 + jnp.exp(-x))


def _silu(x):
    return x * _sigmoid(x)


def _softplus(x):
    return jnp.maximum(x, 0.0) + jnp.log1p(jnp.exp(-jnp.abs(x)))


def _rms(x, gain):
    return x * lax.rsqrt(jnp.mean(x * x, axis=-1, keepdims=True) + EPS) * gain


def _mod_kernel(c_ref, w_ref, b_ref, o_ref):
    a = _silu(c_ref[...])
    o_ref[...] = jnp.dot(a, w_ref[...], preferred_element_type=F32,
                         precision=lax.Precision.HIGHEST) + b_ref[...]


def _modulation(c_all, w_mod, b_mod):
    rows, d = c_all.shape
    n = w_mod.shape[1]
    return pl.pallas_call(
        _mod_kernel,
        grid=(n // d,),
        in_specs=[pl.BlockSpec((rows, d), lambda j: (0, 0)),
                  pl.BlockSpec((d, d), lambda j: (0, j)),
                  pl.BlockSpec((1, d), lambda j: (0, j))],
        out_specs=pl.BlockSpec((rows, d), lambda j: (0, j)),
        out_shape=jax.ShapeDtypeStruct((rows, n), F32),
        compiler_params=_cparams("arbitrary"),
        name="modulation",
    )(c_all, w_mod, b_mod)


def _ctx_kernel(raw_ref, ctx_ref, gain_ref, sh_ref, sc_ref, wkv_ref, sf_ref, sb_ref):
    x = ctx_ref[0]
    lc = x.shape[0]
    hc = _rms(x, gain_ref[...]) * (1.0 + sc_ref[...]) + sh_ref[...]
    kv = jnp.dot(hc.astype(BF16), wkv_ref[...], preferred_element_type=F32)
    pos = lax.broadcasted_iota(jnp.int32, (lc, RET_DK), 0).astype(F32)
    tn = (((0,), (0,)), ((), ()))
    for h in range(RET_HEADS):
        lgf = -_softplus(jnp.full((lc, RET_DK), raw_ref[0, h], F32))
        lgb = -_softplus(jnp.full((lc, RET_DK), raw_ref[1, h], F32))
        k = kv[:, h * RET_DK:(h + 1) * RET_DK] * (RET_DK ** -0.5)
        v = kv[:, (RET_HEADS + h) * RET_DK:(RET_HEADS + h + 1) * RET_DK].astype(BF16)
        kf = (k * jnp.exp(lgf * (lc - 1.0 - pos))).astype(BF16)
        kb = (k * jnp.exp(lgb * pos)).astype(BF16)
        sf_ref[0, h] = lax.dot_general(kf, v, tn, preferred_element_type=F32)
        sb_ref[0, h] = lax.dot_general(kb, v, tn, preferred_element_type=F32)


def _context_states(raw, ctx, gain0, csh, csc, w_kv):
    b, lc, d = ctx.shape
    st = jax.ShapeDtypeStruct((b, RET_HEADS, RET_DK, RET_DK), F32)
    st_spec = pl.BlockSpec((1, RET_HEADS, RET_DK, RET_DK), lambda i: (i, 0, 0, 0))
    vec = pl.BlockSpec((1, d), lambda i: (0, 0))
    return pl.pallas_call(
        _ctx_kernel,
        grid=(b,),
        in_specs=[pl.BlockSpec(memory_space=pltpu.SMEM),
                  pl.BlockSpec((1, lc, d), lambda i: (i, 0, 0)),
                  vec, vec, vec,
                  pl.BlockSpec(w_kv.shape, lambda i: (0, 0))],
        out_specs=[st_spec, st_spec],
        out_shape=[st, st],
        compiler_params=_cparams("arbitrary"),
        name="context_states",
    )(raw, ctx, gain0, csh, csc, w_kv)


IN_TM = 1024
IN_TILES = 8
IN_OUT_TILES = 7


def _inproj_kernel(x_ref, gain_ref, sh_ref, sc_ref, cos_ref, sin_ref, w_ref, o_ref, xn_ref):
    j = pl.program_id(1)

    @pl.when(j == 0)
    def _():
        h = _rms(x_ref[...], gain_ref[...]) * (1.0 + sc_ref[0]) + sh_ref[0]
        xn_ref[...] = h.astype(BF16)

    acc = jnp.dot(xn_ref[...], w_ref[...], preferred_element_type=F32)
    half = RET_DK // 2

    def rotary(scale):
        cos = cos_ref[...]
        sin = sin_ref[...]
        for h in range(RET_HEADS):
            t1 = acc[:, h * RET_DK:h * RET_DK + half]
            t2 = acc[:, h * RET_DK + half:(h + 1) * RET_DK]
            o_ref[:, h * RET_DK:h * RET_DK + half] = ((t1 * cos - t2 * sin) * scale).astype(BF16)
            o_ref[:, h * RET_DK + half:(h + 1) * RET_DK] = ((t1 * sin + t2 * cos) * scale).astype(BF16)

    @pl.when(j == 0)
    def _():
        rotary(1.0)

    @pl.when(j == 1)
    def _():
        rotary(RET_DK ** -0.5)

    @pl.when(j == 2)
    def _():
        o_ref[...] = acc.astype(BF16)

    @pl.when(j == 3)
    def _():
        o_ref[...] = _silu(acc).astype(BF16)

    hw = D_MODEL // 2

    @pl.when(j == 4)
    def _():
        o_ref[:, :hw] = (acc[:, :hw] * _sigmoid(acc[:, hw:])).astype(BF16)

    @pl.when(j == 5)
    def _():
        o_ref[:, hw:] = (acc[:, :hw] * _sigmoid(acc[:, hw:])).astype(BF16)

    @pl.when(j >= 6)
    def _():
        o_ref[...] = _sigmoid(acc).astype(BF16)


def _out_tile(j):
    return jnp.where(j <= 4, j, j - 1)


def _in_projection(x2, gain0, sh1, sc1, cos, sin, w_perm, seq):
    n, d = x2.shape
    tiles_per_seq = seq // IN_TM
    return pl.pallas_call(
        _inproj_kernel,
        grid=(n // IN_TM, IN_TILES),
        in_specs=[pl.BlockSpec((IN_TM, d), lambda i, j: (i, 0)),
                  pl.BlockSpec((1, d), lambda i, j: (0, 0)),
                  pl.BlockSpec((1, 1, d), lambda i, j: (i // tiles_per_seq, 0, 0)),
                  pl.BlockSpec((1, 1, d), lambda i, j: (i // tiles_per_seq, 0, 0)),
                  pl.BlockSpec((IN_TM, RET_DK // 2), lambda i, j: (i % tiles_per_seq, 0)),
                  pl.BlockSpec((IN_TM, RET_DK // 2), lambda i, j: (i % tiles_per_seq, 0)),
                  pl.BlockSpec((d, d), lambda i, j: (0, j))],
        out_specs=pl.BlockSpec((IN_TM, d), lambda i, j: (i, _out_tile(j))),
        out_shape=jax.ShapeDtypeStruct((n, IN_OUT_TILES * d), BF16),
        scratch_shapes=[pltpu.VMEM((IN_TM, d), BF16)],
        compiler_params=_cparams("arbitrary", "arbitrary"),
        name="in_projection",
    )(x2, gain0, sh1, sc1, cos, sin, w_perm)


def _ret_kernel(raw_ref, s0f_ref, s0b_ref, q_ref, k_ref, v_ref, g_ref, o_ref, sf_ref, sb_ref, y_ref):
    hd = pl.program_id(1)
    c = RET_CHUNK
    seq = q_ref.shape[0]
    nc = seq // c
    def log_gamma(direction, shape):
        return -_softplus(jnp.full(shape, raw_ref[direction, hd], F32))

    ri = lax.broadcasted_iota(jnp.int32, (c, c), 0).astype(F32)
    ci = lax.broadcasted_iota(jnp.int32, (c, c), 1).astype(F32)
    dec_f = jnp.where(ri >= ci, jnp.exp(log_gamma(0, (c, c)) * jnp.maximum(ri - ci, 0.0)), 0.0)
    dec_b = jnp.where(ci >= ri, jnp.exp(log_gamma(1, (c, c)) * jnp.maximum(ci - ri, 0.0)), 0.0)
    pos = lax.broadcasted_iota(jnp.int32, (c, RET_DK), 0).astype(F32)
    lgf = log_gamma(0, (c, RET_DK))
    lgb = log_gamma(1, (c, RET_DK))
    qdec_f = jnp.exp(lgf * (pos + 1.0))
    kdec_f = jnp.exp(lgf * (c - 1.0 - pos))
    qdec_b = jnp.exp(lgb * (c - pos))
    kdec_b = jnp.exp(lgb * pos)
    cdec_f = jnp.exp(log_gamma(0, (1, RET_DK)) * c)
    cdec_b = jnp.exp(log_gamma(1, (1, RET_DK)) * c)
    sf_ref[...] = s0f_ref[0, 0]
    sb_ref[...] = s0b_ref[0, 0]
    nt = (((1,), (1,)), ((), ()))
    tn = (((0,), (0,)), ((), ()))

    def chunk(st_ref, row, dec, qdec, kdec, cdec):
        q = q_ref[pl.ds(row, c), :]
        k = k_ref[pl.ds(row, c), :]
        v = v_ref[pl.ds(row, c), :]
        s = lax.dot_general(q, k, nt, preferred_element_type=F32) * dec
        st = st_ref[...]
        y = jnp.dot(s.astype(BF16), v, preferred_element_type=F32)
        y = y + jnp.dot((q.astype(F32) * qdec).astype(BF16), st.astype(BF16), preferred_element_type=F32)
        kd = (k.astype(F32) * kdec).astype(BF16)
        st_ref[...] = st * cdec + lax.dot_general(kd, v, tn, preferred_element_type=F32)
        return y

    def first_half(i, carry):
        rf = pl.multiple_of(i * c, c)
        rb = pl.multiple_of((nc - 1 - i) * c, c)
        y_ref[pl.ds(rf, c), :] = chunk(sf_ref, rf, dec_f, qdec_f, kdec_f, cdec_f)
        y_ref[pl.ds(rb, c), :] = chunk(sb_ref, rb, dec_b, qdec_b, kdec_b, cdec_b)
        return carry

    def second_half(i, carry):
        rf = pl.multiple_of(i * c, c)
        rb = pl.multiple_of((nc - 1 - i) * c, c)
        y_ref[pl.ds(rf, c), :] += chunk(sf_ref, rf, dec_f, qdec_f, kdec_f, cdec_f)
        y_ref[pl.ds(rb, c), :] += chunk(sb_ref, rb, dec_b, qdec_b, kdec_b, cdec_b)
        return carry

    lax.fori_loop(0, nc // 2, first_half, 0)
    lax.fori_loop(nc // 2, nc, second_half, 0)

    def finish(i, carry):
        r = pl.multiple_of(i * c, c)
        y = y_ref[pl.ds(r, c), :]
        yn = y * lax.rsqrt(jnp.mean(y * y, axis=-1, keepdims=True) + EPS)
        o_ref[pl.ds(r, c), :] = (yn * g_ref[pl.ds(r, c), :].astype(F32)).astype(BF16)
        return carry

    lax.fori_loop(0, nc, finish, 0)


def _retention(raw, st_f, st_b, proj, batch, seq):
    n = batch * seq
    st_spec = pl.BlockSpec((1, 1, RET_DK, RET_DK), lambda b, h: (b, h, 0, 0))

    def col(tile):
        return pl.BlockSpec((seq, RET_DK), lambda b, h: (b, tile * RET_HEADS + h))

    return pl.pallas_call(
        _ret_kernel,
        grid=(batch, RET_HEADS),
        in_specs=[pl.BlockSpec(memory_space=pltpu.SMEM), st_spec, st_spec,
                  col(0), col(1), col(2), col(3)],
        out_specs=pl.BlockSpec((seq, RET_DK), lambda b, h: (b, h)),
        out_shape=jax.ShapeDtypeStruct((n, RET_HEADS * RET_DK), BF16),
        scratch_shapes=[pltpu.VMEM((RET_DK, RET_DK), F32), pltpu.VMEM((RET_DK, RET_DK), F32),
                        pltpu.VMEM((seq, RET_DK), F32)],
        compiler_params=_cparams("arbitrary", "arbitrary"),
        name="retention",
    )(raw, st_f, st_b, proj, proj, proj, proj)


CONV_TL = 256
CONV_HALO = 16
CONV_RB = 32


def _conv_kernel(prev_ref, cur_ref, next_ref, cw_ref, cb_ref, lg_ref, lb_ref, o_ref, pad_ref, *, seq):
    i = pl.program_id(0)
    tl = CONV_TL
    has_prev = (i * tl) % seq != 0
    has_next = ((i + 1) * tl) % seq != 0
    pad_ref[0:CONV_HALO, :] = jnp.where(has_prev, prev_ref[...].astype(F32), 0.0)
    pad_ref[CONV_HALO:CONV_HALO + tl, :] = cur_ref[...].astype(F32)
    pad_ref[CONV_HALO + tl:, :] = jnp.where(has_next, next_ref[...].astype(F32), 0.0)
    off = CONV_HALO - CONV_HALF
    for rb in range(tl // CONV_RB):
        base = rb * CONV_RB + off
        acc = pad_ref[base:base + CONV_RB, :] * cw_ref[0:1, :]
        for w in range(1, CONV_WIDTH):
            acc = acc + pad_ref[base + w:base + w + CONV_RB, :] * cw_ref[w:w + 1, :]
        u = acc + cb_ref[...]
        mu = jnp.mean(u, axis=-1, keepdims=True)
        uc = u - mu
        y = uc * lax.rsqrt(jnp.mean(uc * uc, axis=-1, keepdims=True) + EPS) * lg_ref[...] + lb_ref[...]
        o_ref[rb * CONV_RB:(rb + 1) * CONV_RB, :] = _silu(y).astype(BF16)


def _conv_branch(proj, conv_w, conv_b, ln_g, ln_b, seq):
    n = proj.shape[0]
    d = D_MODEL
    glu_tile = 4
    hb = CONV_TL // CONV_HALO
    n_halo = n // CONV_HALO
    vec = pl.BlockSpec((1, d), lambda i: (0, 0))
    return pl.pallas_call(
        functools.partial(_conv_kernel, seq=seq),
        grid=(n // CONV_TL,),
        in_specs=[pl.BlockSpec((CONV_HALO, d), lambda i: (jnp.maximum(i * hb - 1, 0), glu_tile)),
                  pl.BlockSpec((CONV_TL, d), lambda i: (i, glu_tile)),
                  pl.BlockSpec((CONV_HALO, d), lambda i: (jnp.minimum((i + 1) * hb, n_halo - 1), glu_tile)),
                  pl.BlockSpec((CONV_WIDTH, d), lambda i: (0, 0)),
                  vec, vec, vec],
        out_specs=pl.BlockSpec((CONV_TL, d), lambda i: (i, 0)),
        out_shape=jax.ShapeDtypeStruct((n, d), BF16),
        scratch_shapes=[pltpu.VMEM((CONV_TL + 2 * CONV_HALO, d), F32)],
        compiler_params=_cparams("arbitrary"),
        name="conv_branch",
    )(proj, proj, proj, conv_w, conv_b, ln_g, ln_b)


MERGE_TM = 256


def _merge_kernel(x_ref, yr_ref, uc_ref, ga_ref, gb_ref, wr_ref, wc_ref, wo_ref, wq_ref,
                  gain1_ref, gain2_ref, g1_ref, sh2_ref, sc2_ref, lat_ref, h2_ref, qp_ref):
    a = jnp.dot(yr_ref[...], wr_ref[...], preferred_element_type=F32)
    b = jnp.dot(uc_ref[...], wc_ref[...], preferred_element_type=F32)
    merged = ga_ref[...].astype(F32) * a + gb_ref[...].astype(F32) * b
    y = jnp.dot(merged.astype(BF16), wo_ref[...], preferred_element_type=F32)
    lat = x_ref[...] + g1_ref[0] * _rms(y, gain1_ref[...])
    lat_ref[...] = lat
    h2 = _rms(lat, gain2_ref[...]) * (1.0 + sc2_ref[0]) + sh2_ref[0]
    h2_ref[...] = h2
    qp_ref[...] = jnp.dot(h2.astype(BF16), wq_ref[...], preferred_element_type=F32).astype(BF16)


def _merge(x2, y_ret, u_conv, proj, w_ret_o, w_conv_o, w_out, w_q, gain1, gain2, g1, sh2, sc2, seq):
    n, d = x2.shape
    tps = seq // MERGE_TM
    nq = w_q.shape[1]
    row = pl.BlockSpec((MERGE_TM, d), lambda i: (i, 0))
    wsp = pl.BlockSpec((d, d), lambda i: (0, 0))
    vec = pl.BlockSpec((1, d), lambda i: (0, 0))
    bvec = pl.BlockSpec((1, 1, d), lambda i: (i // tps, 0, 0))
    return pl.pallas_call(
        _merge_kernel,
        grid=(n // MERGE_TM,),
        in_specs=[row, row, row,
                  pl.BlockSpec((MERGE_TM, d), lambda i: (i, 5)),
                  pl.BlockSpec((MERGE_TM, d), lambda i: (i, 6)),
                  wsp, wsp, wsp,
                  pl.BlockSpec((d, nq), lambda i: (0, 0)),
                  vec, vec, bvec, bvec, bvec],
        out_specs=[row, row, pl.BlockSpec((MERGE_TM, nq), lambda i: (i, 0))],
        out_shape=[jax.ShapeDtypeStruct((n, d), F32), jax.ShapeDtypeStruct((n, d), F32),
                   jax.ShapeDtypeStruct((n, nq), BF16)],
        compiler_params=_cparams("arbitrary"),
        name="merge",
    )(x2, y_ret, u_conv, proj, proj, w_ret_o, w_conv_o, w_out, w_q, gain1, gain2, g1, sh2, sc2)


ROUTE_T = 512
LANE = 128


def _topk_rows(s, k, payload=None):
    r = s.shape[0]
    rows = lax.broadcasted_iota(jnp.int32, s.shape, 0).astype(F32)
    vals, idxs = [], []
    for _ in range(k):
        m = jnp.max(s, axis=0, keepdims=True)
        pos = jnp.min(jnp.where(s == m, rows, float(r)), axis=0, keepdims=True)
        hit = rows == pos
        vals.append(m)
        if payload is None:
            idxs.append(pos)
        else:
            idxs.append(jnp.max(jnp.where(hit, payload, -1.0), axis=0, keepdims=True))
        s = jnp.where(hit, -jnp.inf, s)
    return jnp.concatenate(vals, axis=0), jnp.concatenate(idxs, axis=0)


def _route_kernel(q_ref, keys_ref, e_ref, g_ref, es_ref, gs_ref):
    h = pl.program_id(1)
    nt = (((1,), (1,)), ((), ()))
    row0 = pl.multiple_of(h * PEER_TOPK, PEER_TOPK)
    for t in range(ROUTE_T // LANE):
        cols = slice(t * LANE, (t + 1) * LANE)
        tops = []
        for p in range(2):
            qh = q_ref[cols, p * PEER_DK_HALF:(p + 1) * PEER_DK_HALF]
            s = lax.dot_general(keys_ref[0, p], qh, nt, preferred_element_type=F32)
            tops.append(_topk_rows(s, PEER_TOPK))
        (v1, i1), (v2, i2) = tops
        cand = jnp.concatenate([v1[a:a + 1] + v2 for a in range(PEER_TOPK)], axis=0)
        cidx = jnp.concatenate([i1[a:a + 1] * float(PEER_N_KEYS) + i2 for a in range(PEER_TOPK)], axis=0)
        best, experts = _topk_rows(cand, PEER_TOPK, payload=cidx)
        ex = jnp.exp(best - best[0:1])
        gates = ex / jnp.sum(ex, axis=0, keepdims=True)
        es_ref[pl.ds(row0, PEER_TOPK), cols] = experts.astype(jnp.int32)
        gs_ref[pl.ds(row0, PEER_TOPK), cols] = gates

    @pl.when(h == PEER_HEADS - 1)
    def _():
        e_ref[...] = es_ref[...].T
        g_ref[...] = gs_ref[...].T


def _route(qp, keys):
    n = qp.shape[0]
    hw = 2 * PEER_DK_HALF
    out = pl.BlockSpec((ROUTE_T, PEER_SEL), lambda i, h: (i, 0))
    return pl.pallas_call(
        _route_kernel,
        grid=(n // ROUTE_T, PEER_HEADS),
        in_specs=[pl.BlockSpec((ROUTE_T, hw), lambda i, h: (i, h)),
                  pl.BlockSpec((1, 2, PEER_N_KEYS, PEER_DK_HALF), lambda i, h: (h, 0, 0, 0))],
        out_specs=[out, out],
        out_shape=[jax.ShapeDtypeStruct((n, PEER_SEL), jnp.int32),
                   jax.ShapeDtypeStruct((n, PEER_SEL), F32)],
        scratch_shapes=[pltpu.VMEM((PEER_SEL, ROUTE_T), jnp.int32), pltpu.VMEM((PEER_SEL, ROUTE_T), F32)],
        compiler_params=_cparams("arbitrary", "arbitrary"),
        name="peer_route",
    )(qp, keys)


def _sc_mesh():
    return plsc.VectorSubcoreMesh(core_axis_name="c", subcore_axis_name="s")


def _worker_id():
    return lax.axis_index("s") * SC_CORES + lax.axis_index("c")


def _peer_scores_sc(table, h2, experts):
    n, d = h2.shape
    _, sub_rows, lane_w = table.shape
    per = n // SC_WORKERS
    nchunk = PEER_SEL // SC_ROWS
    lane_chunks = lane_w // SC_LANES
    ncol = sub_rows * lane_chunks

    @functools.partial(
        pl.kernel, mesh=_sc_mesh(),
        out_type=jax.ShapeDtypeStruct((n, PEER_SEL), F32),
        scratch_types=[pltpu.VMEM((SC_TB, PEER_SEL), jnp.int32), pltpu.VMEM((SC_TB, d), F32),
                       pltpu.VMEM((SC_ROWS, sub_rows, lane_w), F32), pltpu.VMEM((SC_ROWS, sub_rows, lane_w), F32),
                       pltpu.VMEM((SC_TB, PEER_SEL), F32),
                       pltpu.SemaphoreType.DMA, pltpu.SemaphoreType.DMA],
        compiler_params=pltpu.CompilerParams(needs_layout_passes=False),
        name="peer_scores_sc",
    )
    def run(tab_hbm, x_hbm, idx_hbm, out_hbm, idx_v, x_v, rows0, rows1, res_v, sem0, sem1):
        base = _worker_id() * per
        bufs = (rows0, rows1)
        sems = (sem0, sem1)
        lane = lax.iota(jnp.int32, SC_LANES)
        zero = jnp.zeros((SC_LANES,), F32)

        def gather(tl, ch):
            return pltpu.make_async_copy(tab_hbm.at[idx_v.at[tl, pl.ds(ch * SC_ROWS, SC_ROWS)]],
                                         bufs[ch % 2], sems[ch % 2])

        @pl.loop(0, per // SC_TB)
        def _(blk):
            tok0 = pl.multiple_of(base + blk * SC_TB, SC_TB)
            pltpu.sync_copy(idx_hbm.at[pl.ds(tok0, SC_TB)], idx_v)
            pltpu.sync_copy(x_hbm.at[pl.ds(tok0, SC_TB)], x_v)
            gather(0, 0).start()

            @pl.loop(0, SC_TB)
            def _(tl):
                for ch in range(nchunk):
                    if ch + 1 < nchunk:
                        gather(tl, ch + 1).start()
                    else:
                        @pl.when(tl + 1 < SC_TB)
                        def _():
                            gather(tl + 1, 0).start()
                    gather(tl, ch).wait()
                    rows = bufs[ch % 2]

                    def col_step(cidx, accs):
                        sub = lax.shift_right_logical(cidx, lane_chunks.bit_length() - 1)
                        off = pl.multiple_of((cidx & (lane_chunks - 1)) * SC_LANES, SC_LANES)
                        xc = x_v[tl, pl.ds(pl.multiple_of(cidx * SC_LANES, SC_LANES), SC_LANES)]
                        return tuple(accs[r] + rows[r, sub, pl.ds(off, SC_LANES)] * xc for r in range(SC_ROWS))

                    accs = lax.fori_loop(0, ncol, col_step, (zero,) * SC_ROWS)
                    for g in range(SC_ROWS // SC_LANES):
                        res = zero
                        for r in range(SC_LANES):
                            tot = jnp.sum(accs[g * SC_LANES + r])
                            res = jnp.where(lane == r, tot, res)
                        res_v[tl, pl.ds(ch * SC_ROWS + g * SC_LANES, SC_LANES)] = res

            pltpu.sync_copy(res_v, out_hbm.at[pl.ds(tok0, SC_TB)])

    return run(table, h2, experts)


def _peer_combine_sc(table, weights, experts):
    n = weights.shape[0]
    _, sub_rows, lane_w = table.shape
    per = n // SC_WORKERS
    nchunk = PEER_SEL // SC_ROWS
    lane_chunks = lane_w // SC_LANES
    sub_half = sub_rows // 2
    slots = [(s, c * SC_LANES) for s in range(sub_half) for c in range(lane_chunks)]

    @functools.partial(
        pl.kernel, mesh=_sc_mesh(),
        out_type=jax.ShapeDtypeStruct((n, sub_rows * lane_w), F32),
        scratch_types=[pltpu.VMEM((SC_TB, PEER_SEL), jnp.int32), pltpu.VMEM((SC_TB, PEER_SEL), F32),
                       pltpu.VMEM((SC_ROWS, sub_rows, lane_w), F32), pltpu.VMEM((SC_ROWS, sub_rows, lane_w), F32),
                       pltpu.VMEM((SC_TB, sub_rows * lane_w), F32),
                       pltpu.SemaphoreType.DMA, pltpu.SemaphoreType.DMA],
        compiler_params=pltpu.CompilerParams(needs_layout_passes=False),
        name="peer_combine_sc",
    )
    def run(tab_hbm, w_hbm, idx_hbm, out_hbm, idx_v, w_v, rows0, rows1, acc_v, sem0, sem1):
        base = _worker_id() * per
        bufs = (rows0, rows1)
        sems = (sem0, sem1)
        zero = jnp.zeros((SC_LANES,), F32)

        def gather(tl, ch):
            return pltpu.make_async_copy(tab_hbm.at[idx_v.at[tl, pl.ds(ch * SC_ROWS, SC_ROWS)]],
                                         bufs[ch % 2], sems[ch % 2])

        @pl.loop(0, per // SC_TB)
        def _(blk):
            tok0 = pl.multiple_of(base + blk * SC_TB, SC_TB)
            pltpu.sync_copy(idx_hbm.at[pl.ds(tok0, SC_TB)], idx_v)
            pltpu.sync_copy(w_hbm.at[pl.ds(tok0, SC_TB)], w_v)
            gather(0, 0).start()

            @pl.loop(0, SC_TB)
            def _(tl):
                tl_vec = jnp.full((SC_LANES,), 0, jnp.int32) + tl
                for ch in range(nchunk):
                    if ch + 1 < nchunk:
                        gather(tl, ch + 1).start()
                    else:
                        @pl.when(tl + 1 < SC_TB)
                        def _():
                            gather(tl + 1, 0).start()
                    gather(tl, ch).wait()
                    rows = bufs[ch % 2]
                    for hf in range(2):
                        s0 = hf * sub_half
                        if ch == 0:
                            init = (zero,) * len(slots)
                        else:
                            init = tuple(acc_v[tl, pl.ds((s0 + s) * lane_w + c, SC_LANES)] for s, c in slots)

                        def row_step(r, accs):
                            wr = plsc.load_gather(w_v, [tl_vec, jnp.full((SC_LANES,), ch * SC_ROWS, jnp.int32) + r])
                            return tuple(acc + wr * rows[r, s0 + s, pl.ds(c, SC_LANES)]
                                         for acc, (s, c) in zip(accs, slots))

                        accs = lax.fori_loop(0, SC_ROWS, row_step, init)
                        for acc, (s, c) in zip(accs, slots):
                            acc_v[tl, pl.ds((s0 + s) * lane_w + c, SC_LANES)] = acc

            pltpu.sync_copy(acc_v, out_hbm.at[pl.ds(tok0, SC_TB)])

    return run(table, weights, experts)


EW_TM = 2048


def _gelu_tanh(x):
    c = 0.7978845608028654
    return 0.5 * x * (1.0 + jnp.tanh(c * (x + 0.044715 * (x * x * x))))


def _gate_kernel(s_ref, g_ref, o_ref):
    o_ref[...] = g_ref[...] * _gelu_tanh(s_ref[...])


def _gate_weights(scores, gates):
    n, k = scores.shape
    spec = pl.BlockSpec((EW_TM, k), lambda i: (i, 0))
    return pl.pallas_call(
        _gate_kernel, grid=(n // EW_TM,), in_specs=[spec, spec], out_specs=spec,
        out_shape=jax.ShapeDtypeStruct((n, k), F32),
        compiler_params=_cparams("arbitrary"), name="peer_gate",
    )(scores, gates)


FIN_TM = 1024


def _final_kernel(lat_ref, p_ref, gain_ref, g2_ref, o_ref):
    o_ref[...] = lat_ref[...] + g2_ref[0] * _rms(p_ref[...], gain_ref[...])


def _final(lat, peer, gain3, g2, seq):
    n, d = lat.shape
    tps = seq // FIN_TM
    row = pl.BlockSpec((FIN_TM, d), lambda i: (i, 0))
    return pl.pallas_call(
        _final_kernel, grid=(n // FIN_TM,),
        in_specs=[row, row, pl.BlockSpec((1, d), lambda i: (0, 0)),
                  pl.BlockSpec((1, 1, d), lambda i: (i // tps, 0, 0))],
        out_specs=row, out_shape=jax.ShapeDtypeStruct((n, d), F32),
        compiler_params=_cparams("arbitrary"), name="final_residual",
    )(lat, peer, gain3, g2)


def _axial_rotary(seq):
    n_rows = seq // GRID_W
    rows = jnp.repeat(jnp.arange(n_rows, dtype=F32), GRID_W)
    cols = jnp.tile(jnp.arange(GRID_W, dtype=F32), n_rows)
    quarter = RET_DK // 4
    inv = ROPE_BASE ** (-jnp.arange(quarter, dtype=F32) / quarter)
    ang = jnp.concatenate([rows[:, None] * inv, cols[:, None] * inv], axis=-1)
    return jnp.cos(ang), jnp.sin(ang)


def kernel(x, c, ctx, c_ctx, w_mod, b_mod, norm_gain, w_in, ret_decay_raw, w_ret_o, conv_w, conv_b,
           conv_norm_g, conv_norm_b, w_conv_o, w_out, peer_wq, peer_keys, peer_u, peer_v):
    assert w_mod.shape[0] == 1, "single layer"
    batch, seq, d = x.shape
    n = batch * seq
    assert d == D_MODEL and seq % IN_TM == 0 and n % (SC_WORKERS * 8) == 0
    gain = norm_gain[0]
    x2 = x.reshape(n, d)

    pad = (-(batch + 1)) % 8
    c_all = jnp.concatenate([c, c_ctx[None, :], jnp.zeros((pad, d), F32)], axis=0)
    mod = _modulation(c_all, w_mod[0], b_mod[0][None, :])
    mod_lat = mod[:batch].reshape(batch, N_MOD, 1, d)
    sh1, sc1, g1, sh2, sc2, g2 = (mod_lat[:, i] for i in range(N_MOD))
    mod_ctx = mod[batch:batch + 1].reshape(N_MOD, 1, d)

    w_in0 = w_in[0]
    q_w = RET_HEADS * RET_DK
    w_kv = w_in0[:, q_w:3 * q_w].astype(BF16)
    st_f, st_b = _context_states(ret_decay_raw[0], ctx, gain[0:1], mod_ctx[0], mod_ctx[1], w_kv)

    hw = d // 2
    glu_a = w_in0[:, 4 * d:5 * d]
    glu_b = w_in0[:, 5 * d:6 * d]
    w_perm = jnp.concatenate([w_in0[:, :4 * d], glu_a[:, :hw], glu_b[:, :hw], glu_a[:, hw:], glu_b[:, hw:],
                              w_in0[:, 6 * d:]], axis=1).astype(BF16)
    cos, sin = _axial_rotary(seq)
    proj = _in_projection(x2, gain[0:1], sh1, sc1, cos, sin, w_perm, seq)

    y_ret = _retention(ret_decay_raw[0], st_f, st_b, proj, batch, seq)
    u_conv = _conv_branch(proj, conv_w[0], conv_b[0][None, :], conv_norm_g[0][None, :],
                          conv_norm_b[0][None, :], seq)
    lat1, h2, qp = _merge(x2, y_ret, u_conv, proj, w_ret_o[0].astype(BF16), w_conv_o[0].astype(BF16),
                          w_out[0].astype(BF16), peer_wq[0].astype(BF16), gain[1:2], gain[2:3],
                          g1, sh2, sc2, seq)

    experts, gates = _route(qp, peer_keys[0].astype(BF16))
    tile = (SUBLANE, d // SUBLANE)
    scores = _peer_scores_sc(peer_u[0].reshape(-1, *tile), h2, experts)
    weights = _gate_weights(scores, gates)
    peer_out = _peer_combine_sc(peer_v[0].reshape(-1, *tile), weights, experts)
    out = _final(lat1, peer_out, gain[3:4], g2, seq)
    return out.reshape(batch, seq, d)
```

```python
import functools

import jax
import jax.numpy as jnp
from jax import lax
from jax.experimental import pallas as pl
from jax.experimental.pallas import tpu as pltpu
from jax.experimental.pallas import tpu_sc as plsc

F32 = jnp.float32
BF16 = jnp.bfloat16

D_MODEL = 1024
GRID_W = 64
EPS = 1e-6
N_MOD = 6
RET_HEADS = 4
RET_DK = 256
RET_CHUNK = 128
ROPE_BASE = 10000.0
CONV_WIDTH = 31
CONV_HALF = CONV_WIDTH // 2
PEER_HEADS = 8
PEER_N_KEYS = 128
PEER_DK_HALF = 128
PEER_TOPK = 16
PEER_SEL = PEER_HEADS * PEER_TOPK

SC_CORES = 2
SC_SUBCORES = 16
SC_WORKERS = SC_CORES * SC_SUBCORES
SC_LANES = 16
SC_ROWS = 32
SC_TB = 16
SUBLANE = 8

BATCH_GROUPS = 4

VMEM_LIMIT = 48 * 1024 * 1024


def _cparams(*sem):
    return pltpu.CompilerParams(dimension_semantics=sem, vmem_limit_bytes=VMEM_LIMIT)


def _sigmoid(x):
    return 1.0 / (1.0 + jnp.exp(-x))


def _silu(x):
    return x * _sigmoid(x)


def _softplus(x):
    return jnp.maximum(x, 0.0) + jnp.log1p(jnp.exp(-jnp.abs(x)))


def _rms(x, gain):
    return x * lax.rsqrt(jnp.mean(x * x, axis=-1, keepdims=True) + EPS) * gain


def _mod_kernel(c_ref, w_ref, b_ref, o_ref):
    a = _silu(c_ref[...])
    o_ref[...] = jnp.dot(a, w_ref[...], preferred_element_type=F32,
                         precision=lax.Precision.HIGHEST) + b_ref[...]


def _modulation(c_all, w_mod, b_mod):
    rows, d = c_all.shape
    n = w_mod.shape[1]
    return pl.pallas_call(
        _mod_kernel,
        grid=(n // d,),
        in_specs=[pl.BlockSpec((rows, d), lambda j: (0, 0)),
                  pl.BlockSpec((d, d), lambda j: (0, j)),
                  pl.BlockSpec((1, d), lambda j: (0, j))],
        out_specs=pl.BlockSpec((rows, d), lambda j: (0, j)),
        out_shape=jax.ShapeDtypeStruct((rows, n), F32),
        compiler_params=_cparams("arbitrary"),
        name="modulation",
    )(c_all, w_mod, b_mod)


def _ctx_kernel(raw_ref, ctx_ref, gain_ref, sh_ref, sc_ref, wkv_ref, sf_ref, sb_ref):
    x = ctx_ref[0]
    lc = x.shape[0]
    hc = _rms(x, gain_ref[...]) * (1.0 + sc_ref[...]) + sh_ref[...]
    kv = jnp.dot(hc.astype(BF16), wkv_ref[...], preferred_element_type=F32)
    pos = lax.broadcasted_iota(jnp.int32, (lc, RET_DK), 0).astype(F32)
    tn = (((0,), (0,)), ((), ()))
    for h in range(RET_HEADS):
        lgf = -_softplus(jnp.full((lc, RET_DK), raw_ref[0, h], F32))
        lgb = -_softplus(jnp.full((lc, RET_DK), raw_ref[1, h], F32))
        k = kv[:, h * RET_DK:(h + 1) * RET_DK] * (RET_DK ** -0.5)
        v = kv[:, (RET_HEADS + h) * RET_DK:(RET_HEADS + h + 1) * RET_DK].astype(BF16)
        kf = (k * jnp.exp(lgf * (lc - 1.0 - pos))).astype(BF16)
        kb = (k * jnp.exp(lgb * pos)).astype(BF16)
        sf_ref[0, h] = lax.dot_general(kf, v, tn, preferred_element_type=F32)
        sb_ref[0, h] = lax.dot_general(kb, v, tn, preferred_element_type=F32)


def _context_states(raw, ctx, gain0, csh, csc, w_kv):
    b, lc, d = ctx.shape
    st = jax.ShapeDtypeStruct((b, RET_HEADS, RET_DK, RET_DK), F32)
    st_spec = pl.BlockSpec((1, RET_HEADS, RET_DK, RET_DK), lambda i: (i, 0, 0, 0))
    vec = pl.BlockSpec((1, d), lambda i: (0, 0))
    return pl.pallas_call(
        _ctx_kernel,
        grid=(b,),
        in_specs=[pl.BlockSpec(memory_space=pltpu.SMEM),
                  pl.BlockSpec((1, lc, d), lambda i: (i, 0, 0)),
                  vec, vec, vec,
                  pl.BlockSpec(w_kv.shape, lambda i: (0, 0))],
        out_specs=[st_spec, st_spec],
        out_shape=[st, st],
        compiler_params=_cparams("arbitrary"),
        name="context_states",
    )(raw, ctx, gain0, csh, csc, w_kv)


IN_TM = 1024
IN_TILES = 8
IN_OUT_TILES = 7


def _inproj_kernel(x_ref, gain_ref, sh_ref, sc_ref, cos_ref, sin_ref, w_ref, o_ref, xn_ref):
    j = pl.program_id(1)

    @pl.when(j == 0)
    def _():
        h = _rms(x_ref[...], gain_ref[...]) * (1.0 + sc_ref[0]) + sh_ref[0]
        xn_ref[...] = h.astype(BF16)

    acc = jnp.dot(xn_ref[...], w_ref[...], preferred_element_type=F32)
    half = RET_DK // 2

    def rotary(scale):
        cos = cos_ref[...]
        sin = sin_ref[...]
        for h in range(RET_HEADS):
            t1 = acc[:, h * RET_DK:h * RET_DK + half]
            t2 = acc[:, h * RET_DK + half:(h + 1) * RET_DK]
            o_ref[:, h * RET_DK:h * RET_DK + half] = ((t1 * cos - t2 * sin) * scale).astype(BF16)
            o_ref[:, h * RET_DK + half:(h + 1) * RET_DK] = ((t1 * sin + t2 * cos) * scale).astype(BF16)

    @pl.when(j == 0)
    def _():
        rotary(1.0)

    @pl.when(j == 1)
    def _():
        rotary(RET_DK ** -0.5)

    @pl.when(j == 2)
    def _():
        o_ref[...] = acc.astype(BF16)

    @pl.when(j == 3)
    def _():
        o_ref[...] = _silu(acc).astype(BF16)

    hw = D_MODEL // 2

    @pl.when(j == 4)
    def _():
        o_ref[:, :hw] = (acc[:, :hw] * _sigmoid(acc[:, hw:])).astype(BF16)

    @pl.when(j == 5)
    def _():
        o_ref[:, hw:] = (acc[:, :hw] * _sigmoid(acc[:, hw:])).astype(BF16)

    @pl.when(j >= 6)
    def _():
        o_ref[...] = _sigmoid(acc).astype(BF16)


def _out_tile(j):
    return jnp.where(j <= 4, j, j - 1)


def _in_projection(x2, gain0, sh1, sc1, cos, sin, w_perm, seq):
    n, d = x2.shape
    tiles_per_seq = seq // IN_TM
    return pl.pallas_call(
        _inproj_kernel,
        grid=(n // IN_TM, IN_TILES),
        in_specs=[pl.BlockSpec((IN_TM, d), lambda i, j: (i, 0)),
                  pl.BlockSpec((1, d), lambda i, j: (0, 0)),
                  pl.BlockSpec((1, 1, d), lambda i, j: (i // tiles_per_seq, 0, 0)),
                  pl.BlockSpec((1, 1, d), lambda i, j: (i // tiles_per_seq, 0, 0)),
                  pl.BlockSpec((IN_TM, RET_DK // 2), lambda i, j: (i % tiles_per_seq, 0)),
                  pl.BlockSpec((IN_TM, RET_DK // 2), lambda i, j: (i % tiles_per_seq, 0)),
                  pl.BlockSpec((d, d), lambda i, j: (0, j))],
        out_specs=pl.BlockSpec((IN_TM, d), lambda i, j: (i, _out_tile(j))),
        out_shape=jax.ShapeDtypeStruct((n, IN_OUT_TILES * d), BF16),
        scratch_shapes=[pltpu.VMEM((IN_TM, d), BF16)],
        compiler_params=_cparams("arbitrary", "arbitrary"),
        name="in_projection",
    )(x2, gain0, sh1, sc1, cos, sin, w_perm)


def _ret_kernel(raw_ref, s0f_ref, s0b_ref, q_ref, k_ref, v_ref, g_ref, o_ref, sf_ref, sb_ref, y_ref):
    hd = pl.program_id(1)
    c = RET_CHUNK
    seq = q_ref.shape[0]
    nc = seq // c
    def log_gamma(direction, shape):
        return -_softplus(jnp.full(shape, raw_ref[direction, hd], F32))

    ri = lax.broadcasted_iota(jnp.int32, (c, c), 0).astype(F32)
    ci = lax.broadcasted_iota(jnp.int32, (c, c), 1).astype(F32)
    dec_f = jnp.where(ri >= ci, jnp.exp(log_gamma(0, (c, c)) * jnp.maximum(ri - ci, 0.0)), 0.0)
    dec_b = jnp.where(ci >= ri, jnp.exp(log_gamma(1, (c, c)) * jnp.maximum(ci - ri, 0.0)), 0.0)
    pos = lax.broadcasted_iota(jnp.int32, (c, RET_DK), 0).astype(F32)
    lgf = log_gamma(0, (c, RET_DK))
    lgb = log_gamma(1, (c, RET_DK))
    qdec_f = jnp.exp(lgf * (pos + 1.0))
    kdec_f = jnp.exp(lgf * (c - 1.0 - pos))
    qdec_b = jnp.exp(lgb * (c - pos))
    kdec_b = jnp.exp(lgb * pos)
    cdec_f = jnp.exp(log_gamma(0, (1, RET_DK)) * c)
    cdec_b = jnp.exp(log_gamma(1, (1, RET_DK)) * c)
    sf_ref[...] = s0f_ref[0, 0]
    sb_ref[...] = s0b_ref[0, 0]
    nt = (((1,), (1,)), ((), ()))
    tn = (((0,), (0,)), ((), ()))

    def chunk(st_ref, row, dec, qdec, kdec, cdec):
        q = q_ref[pl.ds(row, c), :]
        k = k_ref[pl.ds(row, c), :]
        v = v_ref[pl.ds(row, c), :]
        s = lax.dot_general(q, k, nt, preferred_element_type=F32) * dec
        st = st_ref[...]
        y = jnp.dot(s.astype(BF16), v, preferred_element_type=F32)
        y = y + jnp.dot((q.astype(F32) * qdec).astype(BF16), st.astype(BF16), preferred_element_type=F32)
        kd = (k.astype(F32) * kdec).astype(BF16)
        st_ref[...] = st * cdec + lax.dot_general(kd, v, tn, preferred_element_type=F32)
        return y

    def first_half(i, carry):
        rf = pl.multiple_of(i * c, c)
        rb = pl.multiple_of((nc - 1 - i) * c, c)
        y_ref[pl.ds(rf, c), :] = chunk(sf_ref, rf, dec_f, qdec_f, kdec_f, cdec_f)
        y_ref[pl.ds(rb, c), :] = chunk(sb_ref, rb, dec_b, qdec_b, kdec_b, cdec_b)
        return carry

    def second_half(i, carry):
        rf = pl.multiple_of(i * c, c)
        rb = pl.multiple_of((nc - 1 - i) * c, c)
        y_ref[pl.ds(rf, c), :] += chunk(sf_ref, rf, dec_f, qdec_f, kdec_f, cdec_f)
        y_ref[pl.ds(rb, c), :] += chunk(sb_ref, rb, dec_b, qdec_b, kdec_b, cdec_b)
        return carry

    lax.fori_loop(0, nc // 2, first_half, 0)
    lax.fori_loop(nc // 2, nc, second_half, 0)

    def finish(i, carry):
        r = pl.multiple_of(i * c, c)
        y = y_ref[pl.ds(r, c), :]
        yn = y * lax.rsqrt(jnp.mean(y * y, axis=-1, keepdims=True) + EPS)
        o_ref[pl.ds(r, c), :] = (yn * g_ref[pl.ds(r, c), :].astype(F32)).astype(BF16)
        return carry

    lax.fori_loop(0, nc, finish, 0)


def _retention(raw, st_f, st_b, proj, batch, seq):
    n = batch * seq
    st_spec = pl.BlockSpec((1, 1, RET_DK, RET_DK), lambda b, h: (b, h, 0, 0))

    def col(tile):
        return pl.BlockSpec((seq, RET_DK), lambda b, h: (b, tile * RET_HEADS + h))

    return pl.pallas_call(
        _ret_kernel,
        grid=(batch, RET_HEADS),
        in_specs=[pl.BlockSpec(memory_space=pltpu.SMEM), st_spec, st_spec,
                  col(0), col(1), col(2), col(3)],
        out_specs=pl.BlockSpec((seq, RET_DK), lambda b, h: (b, h)),
        out_shape=jax.ShapeDtypeStruct((n, RET_HEADS * RET_DK), BF16),
        scratch_shapes=[pltpu.VMEM((RET_DK, RET_DK), F32), pltpu.VMEM((RET_DK, RET_DK), F32),
                        pltpu.VMEM((seq, RET_DK), F32)],
        compiler_params=_cparams("arbitrary", "arbitrary"),
        name="retention",
    )(raw, st_f, st_b, proj, proj, proj, proj)


CONV_TL = 256
CONV_HALO = 16
CONV_RB = 32


def _conv_kernel(prev_ref, cur_ref, next_ref, cw_ref, cb_ref, lg_ref, lb_ref, o_ref, pad_ref, *, seq):
    i = pl.program_id(0)
    tl = CONV_TL
    has_prev = (i * tl) % seq != 0
    has_next = ((i + 1) * tl) % seq != 0
    pad_ref[0:CONV_HALO, :] = jnp.where(has_prev, prev_ref[...].astype(F32), 0.0)
    pad_ref[CONV_HALO:CONV_HALO + tl, :] = cur_ref[...].astype(F32)
    pad_ref[CONV_HALO + tl:, :] = jnp.where(has_next, next_ref[...].astype(F32), 0.0)
    off = CONV_HALO - CONV_HALF
    for rb in range(tl // CONV_RB):
        base = rb * CONV_RB + off
        acc = pad_ref[base:base + CONV_RB, :] * cw_ref[0:1, :]
        for w in range(1, CONV_WIDTH):
            acc = acc + pad_ref[base + w:base + w + CONV_RB, :] * cw_ref[w:w + 1, :]
        u = acc + cb_ref[...]
        mu = jnp.mean(u, axis=-1, keepdims=True)
        uc = u - mu
        y = uc * lax.rsqrt(jnp.mean(uc * uc, axis=-1, keepdims=True) + EPS) * lg_ref[...] + lb_ref[...]
        o_ref[rb * CONV_RB:(rb + 1) * CONV_RB, :] = _silu(y).astype(BF16)


def _conv_branch(proj, conv_w, conv_b, ln_g, ln_b, seq):
    n = proj.shape[0]
    d = D_MODEL
    glu_tile = 4
    hb = CONV_TL // CONV_HALO
    n_halo = n // CONV_HALO
    vec = pl.BlockSpec((1, d), lambda i: (0, 0))
    return pl.pallas_call(
        functools.partial(_conv_kernel, seq=seq),
        grid=(n // CONV_TL,),
        in_specs=[pl.BlockSpec((CONV_HALO, d), lambda i: (jnp.maximum(i * hb - 1, 0), glu_tile)),
                  pl.BlockSpec((CONV_TL, d), lambda i: (i, glu_tile)),
                  pl.BlockSpec((CONV_HALO, d), lambda i: (jnp.minimum((i + 1) * hb, n_halo - 1), glu_tile)),
                  pl.BlockSpec((CONV_WIDTH, d), lambda i: (0, 0)),
                  vec, vec, vec],
        out_specs=pl.BlockSpec((CONV_TL, d), lambda i: (i, 0)),
        out_shape=jax.ShapeDtypeStruct((n, d), BF16),
        scratch_shapes=[pltpu.VMEM((CONV_TL + 2 * CONV_HALO, d), F32)],
        compiler_params=_cparams("arbitrary"),
        name="conv_branch",
    )(proj, proj, proj, conv_w, conv_b, ln_g, ln_b)


MERGE_TM = 256


def _merge_kernel(x_ref, yr_ref, uc_ref, ga_ref, gb_ref, wr_ref, wc_ref, wo_ref, wq_ref,
                  gain1_ref, gain2_ref, g1_ref, sh2_ref, sc2_ref, lat_ref, h2_ref, qp_ref):
    a = jnp.dot(yr_ref[...], wr_ref[...], preferred_element_type=F32)
    b = jnp.dot(uc_ref[...], wc_ref[...], preferred_element_type=F32)
    merged = ga_ref[...].astype(F32) * a + gb_ref[...].astype(F32) * b
    y = jnp.dot(merged.astype(BF16), wo_ref[...], preferred_element_type=F32)
    lat = x_ref[...] + g1_ref[0] * _rms(y, gain1_ref[...])
    lat_ref[...] = lat
    h2 = _rms(lat, gain2_ref[...]) * (1.0 + sc2_ref[0]) + sh2_ref[0]
    h2_ref[...] = h2
    qp_ref[...] = jnp.dot(h2.astype(BF16), wq_ref[...], preferred_element_type=F32).astype(BF16)


def _merge(x2, y_ret, u_conv, proj, w_ret_o, w_conv_o, w_out, w_q, gain1, gain2, g1, sh2, sc2, seq):
    n, d = x2.shape
    tps = seq // MERGE_TM
    nq = w_q.shape[1]
    row = pl.BlockSpec((MERGE_TM, d), lambda i: (i, 0))
    wsp = pl.BlockSpec((d, d), lambda i: (0, 0))
    vec = pl.BlockSpec((1, d), lambda i: (0, 0))
    bvec = pl.BlockSpec((1, 1, d), lambda i: (i // tps, 0, 0))
    return pl.pallas_call(
        _merge_kernel,
        grid=(n // MERGE_TM,),
        in_specs=[row, row, row,
                  pl.BlockSpec((MERGE_TM, d), lambda i: (i, 5)),
                  pl.BlockSpec((MERGE_TM, d), lambda i: (i, 6)),
                  wsp, wsp, wsp,
                  pl.BlockSpec((d, nq), lambda i: (0, 0)),
                  vec, vec, bvec, bvec, bvec],
        out_specs=[row, row, pl.BlockSpec((MERGE_TM, nq), lambda i: (i, 0))],
        out_shape=[jax.ShapeDtypeStruct((n, d), F32), jax.ShapeDtypeStruct((n, d), F32),
                   jax.ShapeDtypeStruct((n, nq), BF16)],
        compiler_params=_cparams("arbitrary"),
        name="merge",
    )(x2, y_ret, u_conv, proj, proj, w_ret_o, w_conv_o, w_out, w_q, gain1, gain2, g1, sh2, sc2)


ROUTE_T = 512
LANE = 128


def _topk_rows(s, k, payload=None):
    r = s.shape[0]
    rows = lax.broadcasted_iota(jnp.int32, s.shape, 0).astype(F32)
    vals, idxs = [], []
    for _ in range(k):
        m = jnp.max(s, axis=0, keepdims=True)
        pos = jnp.min(jnp.where(s == m, rows, float(r)), axis=0, keepdims=True)
        hit = rows == pos
        vals.append(m)
        if payload is None:
            idxs.append(pos)
        else:
            idxs.append(jnp.max(jnp.where(hit, payload, -1.0), axis=0, keepdims=True))
        s = jnp.where(hit, -jnp.inf, s)
    return jnp.concatenate(vals, axis=0), jnp.concatenate(idxs, axis=0)


def _route_kernel(q_ref, keys_ref, e_ref, g_ref, es_ref, gs_ref):
    h = pl.program_id(1)
    nt = (((1,), (1,)), ((), ()))
    row0 = pl.multiple_of(h * PEER_TOPK, PEER_TOPK)
    for t in range(ROUTE_T // LANE):
        cols = slice(t * LANE, (t + 1) * LANE)
        tops = []
        for p in range(2):
            qh = q_ref[cols, p * PEER_DK_HALF:(p + 1) * PEER_DK_HALF]
            s = lax.dot_general(keys_ref[0, p], qh, nt, preferred_element_type=F32)
            tops.append(_topk_rows(s, PEER_TOPK))
        (v1, i1), (v2, i2) = tops
        cand = jnp.concatenate([v1[a:a + 1] + v2 for a in range(PEER_TOPK)], axis=0)
        cidx = jnp.concatenate([i1[a:a + 1] * float(PEER_N_KEYS) + i2 for a in range(PEER_TOPK)], axis=0)
        best, experts = _topk_rows(cand, PEER_TOPK, payload=cidx)
        ex = jnp.exp(best - best[0:1])
        gates = ex / jnp.sum(ex, axis=0, keepdims=True)
        es_ref[pl.ds(row0, PEER_TOPK), cols] = experts.astype(jnp.int32)
        gs_ref[pl.ds(row0, PEER_TOPK), cols] = gates

    @pl.when(h == PEER_HEADS - 1)
    def _():
        e_ref[...] = es_ref[...].T
        g_ref[...] = gs_ref[...].T


def _route(qp, keys):
    n = qp.shape[0]
    hw = 2 * PEER_DK_HALF
    out = pl.BlockSpec((ROUTE_T, PEER_SEL), lambda i, h: (i, 0))
    return pl.pallas_call(
        _route_kernel,
        grid=(n // ROUTE_T, PEER_HEADS),
        in_specs=[pl.BlockSpec((ROUTE_T, hw), lambda i, h: (i, h)),
                  pl.BlockSpec((1, 2, PEER_N_KEYS, PEER_DK_HALF), lambda i, h: (h, 0, 0, 0))],
        out_specs=[out, out],
        out_shape=[jax.ShapeDtypeStruct((n, PEER_SEL), jnp.int32),
                   jax.ShapeDtypeStruct((n, PEER_SEL), F32)],
        scratch_shapes=[pltpu.VMEM((PEER_SEL, ROUTE_T), jnp.int32), pltpu.VMEM((PEER_SEL, ROUTE_T), F32)],
        compiler_params=_cparams("arbitrary", "arbitrary"),
        name="peer_route",
    )(qp, keys)


def _sc_mesh():
    return plsc.VectorSubcoreMesh(core_axis_name="c", subcore_axis_name="s")


def _worker_id():
    return lax.axis_index("s") * SC_CORES + lax.axis_index("c")


def _peer_scores_sc(table, h2, experts):
    n, d = h2.shape
    _, sub_rows, lane_w = table.shape
    per = n // SC_WORKERS
    nchunk = PEER_SEL // SC_ROWS
    lane_chunks = lane_w // SC_LANES
    ncol = sub_rows * lane_chunks

    @functools.partial(
        pl.kernel, mesh=_sc_mesh(),
        out_type=jax.ShapeDtypeStruct((n, PEER_SEL), F32),
        scratch_types=[pltpu.VMEM((SC_TB, PEER_SEL), jnp.int32), pltpu.VMEM((SC_TB, d), F32),
                       pltpu.VMEM((SC_ROWS, sub_rows, lane_w), F32), pltpu.VMEM((SC_ROWS, sub_rows, lane_w), F32),
                       pltpu.VMEM((SC_TB, PEER_SEL), F32),
                       pltpu.SemaphoreType.DMA, pltpu.SemaphoreType.DMA],
        compiler_params=pltpu.CompilerParams(needs_layout_passes=False),
        name="peer_scores_sc",
    )
    def run(tab_hbm, x_hbm, idx_hbm, out_hbm, idx_v, x_v, rows0, rows1, res_v, sem0, sem1):
        base = _worker_id() * per
        bufs = (rows0, rows1)
        sems = (sem0, sem1)
        lane = lax.iota(jnp.int32, SC_LANES)
        zero = jnp.zeros((SC_LANES,), F32)

        def gather(tl, ch):
            return pltpu.make_async_copy(tab_hbm.at[idx_v.at[tl, pl.ds(ch * SC_ROWS, SC_ROWS)]],
                                         bufs[ch % 2], sems[ch % 2])

        @pl.loop(0, per // SC_TB)
        def _(blk):
            tok0 = pl.multiple_of(base + blk * SC_TB, SC_TB)
            pltpu.sync_copy(idx_hbm.at[pl.ds(tok0, SC_TB)], idx_v)
            pltpu.sync_copy(x_hbm.at[pl.ds(tok0, SC_TB)], x_v)
            gather(0, 0).start()

            @pl.loop(0, SC_TB)
            def _(tl):
                for ch in range(nchunk):
                    if ch + 1 < nchunk:
                        gather(tl, ch + 1).start()
                    else:
                        @pl.when(tl + 1 < SC_TB)
                        def _():
                            gather(tl + 1, 0).start()
                    gather(tl, ch).wait()
                    rows = bufs[ch % 2]

                    def col_step(cidx, accs):
                        sub = lax.shift_right_logical(cidx, lane_chunks.bit_length() - 1)
                        off = pl.multiple_of((cidx & (lane_chunks - 1)) * SC_LANES, SC_LANES)
                        xc = x_v[tl, pl.ds(pl.multiple_of(cidx * SC_LANES, SC_LANES), SC_LANES)]
                        return tuple(accs[r] + rows[r, sub, pl.ds(off, SC_LANES)] * xc for r in range(SC_ROWS))

                    accs = lax.fori_loop(0, ncol, col_step, (zero,) * SC_ROWS)
                    for g in range(SC_ROWS // SC_LANES):
                        res = zero
                        for r in range(SC_LANES):
                            tot = jnp.sum(accs[g * SC_LANES + r])
                            res = jnp.where(lane == r, tot, res)
                        res_v[tl, pl.ds(ch * SC_ROWS + g * SC_LANES, SC_LANES)] = res

            pltpu.sync_copy(res_v, out_hbm.at[pl.ds(tok0, SC_TB)])

    return run(table, h2, experts)


def _peer_combine_sc(table, weights, experts):
    n = weights.shape[0]
    _, sub_rows, lane_w = table.shape
    per = n // SC_WORKERS
    nchunk = PEER_SEL // SC_ROWS
    lane_chunks = lane_w // SC_LANES
    sub_half = sub_rows // 2
    slots = [(s, c * SC_LANES) for s in range(sub_half) for c in range(lane_chunks)]

    @functools.partial(
        pl.kernel, mesh=_sc_mesh(),
        out_type=jax.ShapeDtypeStruct((n, sub_rows * lane_w), F32),
        scratch_types=[pltpu.VMEM((SC_TB, PEER_SEL), jnp.int32), pltpu.VMEM((SC_TB, PEER_SEL), F32),
                       pltpu.VMEM((SC_ROWS, sub_rows, lane_w), F32), pltpu.VMEM((SC_ROWS, sub_rows, lane_w), F32),
                       pltpu.VMEM((SC_TB, sub_rows * lane_w), F32),
                       pltpu.SemaphoreType.DMA, pltpu.SemaphoreType.DMA],
        compiler_params=pltpu.CompilerParams(needs_layout_passes=False),
        name="peer_combine_sc",
    )
    def run(tab_hbm, w_hbm, idx_hbm, out_hbm, idx_v, w_v, rows0, rows1, acc_v, sem0, sem1):
        base = _worker_id() * per
        bufs = (rows0, rows1)
        sems = (sem0, sem1)
        zero = jnp.zeros((SC_LANES,), F32)

        def gather(tl, ch):
            return pltpu.make_async_copy(tab_hbm.at[idx_v.at[tl, pl.ds(ch * SC_ROWS, SC_ROWS)]],
                                         bufs[ch % 2], sems[ch % 2])

        @pl.loop(0, per // SC_TB)
        def _(blk):
            tok0 = pl.multiple_of(base + blk * SC_TB, SC_TB)
            pltpu.sync_copy(idx_hbm.at[pl.ds(tok0, SC_TB)], idx_v)
            pltpu.sync_copy(w_hbm.at[pl.ds(tok0, SC_TB)], w_v)
            gather(0, 0).start()

            @pl.loop(0, SC_TB)
            def _(tl):
                tl_vec = jnp.full((SC_LANES,), 0, jnp.int32) + tl
                for ch in range(nchunk):
                    if ch + 1 < nchunk:
                        gather(tl, ch + 1).start()
                    else:
                        @pl.when(tl + 1 < SC_TB)
                        def _():
                            gather(tl + 1, 0).start()
                    gather(tl, ch).wait()
                    rows = bufs[ch % 2]
                    for hf in range(2):
                        s0 = hf * sub_half
                        if ch == 0:
                            init = (zero,) * len(slots)
                        else:
                            init = tuple(acc_v[tl, pl.ds((s0 + s) * lane_w + c, SC_LANES)] for s, c in slots)

                        def row_step(r, accs):
                            wr = plsc.load_gather(w_v, [tl_vec, jnp.full((SC_LANES,), ch * SC_ROWS, jnp.int32) + r])
                            return tuple(acc + wr * rows[r, s0 + s, pl.ds(c, SC_LANES)]
                                         for acc, (s, c) in zip(accs, slots))

                        accs = lax.fori_loop(0, SC_ROWS, row_step, init)
                        for acc, (s, c) in zip(accs, slots):
                            acc_v[tl, pl.ds((s0 + s) * lane_w + c, SC_LANES)] = acc

            pltpu.sync_copy(acc_v, out_hbm.at[pl.ds(tok0, SC_TB)])

    return run(table, weights, experts)


EW_TM = 2048


def _gelu_tanh(x):
    c = 0.7978845608028654
    return 0.5 * x * (1.0 + jnp.tanh(c * (x + 0.044715 * (x * x * x))))


def _gate_kernel(s_ref, g_ref, o_ref):
    o_ref[...] = g_ref[...] * _gelu_tanh(s_ref[...])


def _gate_weights(scores, gates):
    n, k = scores.shape
    spec = pl.BlockSpec((EW_TM, k), lambda i: (i, 0))
    return pl.pallas_call(
        _gate_kernel, grid=(n // EW_TM,), in_specs=[spec, spec], out_specs=spec,
        out_shape=jax.ShapeDtypeStruct((n, k), F32),
        compiler_params=_cparams("arbitrary"), name="peer_gate",
    )(scores, gates)


FIN_TM = 1024


def _final_kernel(lat_ref, p_ref, gain_ref, g2_ref, o_ref):
    o_ref[...] = lat_ref[...] + g2_ref[0] * _rms(p_ref[...], gain_ref[...])


def _final(lat, peer, gain3, g2, seq):
    n, d = lat.shape
    tps = seq // FIN_TM
    row = pl.BlockSpec((FIN_TM, d), lambda i: (i, 0))
    return pl.pallas_call(
        _final_kernel, grid=(n // FIN_TM,),
        in_specs=[row, row, pl.BlockSpec((1, d), lambda i: (0, 0)),
                  pl.BlockSpec((1, 1, d), lambda i: (i // tps, 0, 0))],
        out_specs=row, out_shape=jax.ShapeDtypeStruct((n, d), F32),
        compiler_params=_cparams("arbitrary"), name="final_residual",
    )(lat, peer, gain3, g2)


def _axial_rotary(seq):
    n_rows = seq // GRID_W
    rows = jnp.repeat(jnp.arange(n_rows, dtype=F32), GRID_W)
    cols = jnp.tile(jnp.arange(GRID_W, dtype=F32), n_rows)
    quarter = RET_DK // 4
    inv = ROPE_BASE ** (-jnp.arange(quarter, dtype=F32) / quarter)
    ang = jnp.concatenate([rows[:, None] * inv, cols[:, None] * inv], axis=-1)
    return jnp.cos(ang), jnp.sin(ang)


def kernel(x, c, ctx, c_ctx, w_mod, b_mod, norm_gain, w_in, ret_decay_raw, w_ret_o, conv_w, conv_b,
           conv_norm_g, conv_norm_b, w_conv_o, w_out, peer_wq, peer_keys, peer_u, peer_v):
    assert w_mod.shape[0] == 1, "single layer"
    batch, seq, d = x.shape
    n = batch * seq
    assert d == D_MODEL and seq % IN_TM == 0 and seq % (SC_WORKERS * SC_TB) == 0
    gain = norm_gain[0]
    x2 = x.reshape(n, d)

    pad = (-(batch + 1)) % 8
    c_all = jnp.concatenate([c, c_ctx[None, :], jnp.zeros((pad, d), F32)], axis=0)
    mod = _modulation(c_all, w_mod[0], b_mod[0][None, :])
    mod_lat = mod[:batch].reshape(batch, N_MOD, 1, d)
    sh1, sc1, g1, sh2, sc2, g2 = (mod_lat[:, i] for i in range(N_MOD))
    mod_ctx = mod[batch:batch + 1].reshape(N_MOD, 1, d)

    w_in0 = w_in[0]
    q_w = RET_HEADS * RET_DK
    w_kv = w_in0[:, q_w:3 * q_w].astype(BF16)
    st_f, st_b = _context_states(ret_decay_raw[0], ctx, gain[0:1], mod_ctx[0], mod_ctx[1], w_kv)

    hw = d // 2
    glu_a = w_in0[:, 4 * d:5 * d]
    glu_b = w_in0[:, 5 * d:6 * d]
    w_perm = jnp.concatenate([w_in0[:, :4 * d], glu_a[:, :hw], glu_b[:, :hw], glu_a[:, hw:], glu_b[:, hw:],
                              w_in0[:, 6 * d:]], axis=1).astype(BF16)
    cos, sin = _axial_rotary(seq)
    w_ret_b, w_conv_b, w_out_b, w_q_b = (w[0].astype(BF16) for w in (w_ret_o, w_conv_o, w_out, peer_wq))
    keys_b = peer_keys[0].astype(BF16)
    tile = (SUBLANE, d // SUBLANE)
    u_tiles = peer_u[0].reshape(-1, *tile)
    v_tiles = peer_v[0].reshape(-1, *tile)
    conv_vecs = (conv_b[0][None, :], conv_norm_g[0][None, :], conv_norm_b[0][None, :])

    groups = BATCH_GROUPS if batch % BATCH_GROUPS == 0 else 1
    gb = batch // groups
    for tile_rows in (IN_TM, CONV_TL, MERGE_TM, ROUTE_T, EW_TM, FIN_TM, SC_WORKERS * SC_TB):
        assert (gb * seq) % tile_rows == 0, "token count per batch group must be a multiple of every row tile"
    outs = []
    for g in range(groups):
        bs = slice(g * gb, (g + 1) * gb)
        xg = x2[g * gb * seq:(g + 1) * gb * seq]
        proj = _in_projection(xg, gain[0:1], sh1[bs], sc1[bs], cos, sin, w_perm, seq)
        y_ret = _retention(ret_decay_raw[0], st_f[bs], st_b[bs], proj, gb, seq)
        u_conv = _conv_branch(proj, conv_w[0], *conv_vecs, seq)
        lat1, h2, qp = _merge(xg, y_ret, u_conv, proj, w_ret_b, w_conv_b, w_out_b, w_q_b, gain[1:2], gain[2:3],
                              g1[bs], sh2[bs], sc2[bs], seq)
        experts, gates = _route(qp, keys_b)
        scores = _peer_scores_sc(u_tiles, h2, experts)
        weights = _gate_weights(scores, gates)
        peer_out = _peer_combine_sc(v_tiles, weights, experts)
        outs.append(_final(lat1, peer_out, gain[3:4], g2[bs], seq))
    return jnp.concatenate(outs, axis=0).reshape(batch, seq, d)
```

```python
import functools

import jax
import jax.numpy as jnp
from jax import lax
from jax.experimental import pallas as pl
from jax.experimental.pallas import tpu as pltpu
from jax.experimental.pallas import tpu_sc as plsc

F32 = jnp.float32
BF16 = jnp.bfloat16

D_MODEL = 1024
GRID_W = 64
EPS = 1e-6
N_MOD = 6
RET_HEADS = 4
RET_DK = 256
RET_CHUNK = 128
ROPE_BASE = 10000.0
CONV_WIDTH = 31
CONV_HALF = CONV_WIDTH // 2
PEER_HEADS = 8
PEER_N_KEYS = 128
PEER_DK_HALF = 128
PEER_TOPK = 16
PEER_SEL = PEER_HEADS * PEER_TOPK

SC_CORES = 2
SC_SUBCORES = 16
SC_WORKERS = SC_CORES * SC_SUBCORES
SC_LANES = 16
SC_ROWS = 32
SC_TB = 16
LANE = 128
PACK_BLOCK = 2 * LANE
PIECES = D_MODEL // PACK_BLOCK
HIGH_HALF = -65536

BATCH_GROUPS = 4

VMEM_LIMIT = 48 * 1024 * 1024


def _cparams(*sem):
    return pltpu.CompilerParams(dimension_semantics=sem, vmem_limit_bytes=VMEM_LIMIT)


def _sigmoid(x):
    return 1.0 / (1.0 + jnp.exp(-x))


def _silu(x):
    return x * _sigmoid(x)


def _softplus(x):
    return jnp.maximum(x, 0.0) + jnp.log1p(jnp.exp(-jnp.abs(x)))


def _rms(x, gain):
    return x * lax.rsqrt(jnp.mean(x * x, axis=-1, keepdims=True) + EPS) * gain


def _bf16_bits_high(x):
    return lax.bitcast_convert_type(x.astype(BF16).astype(F32), jnp.int32)


def _pack_bf16_pair(lo, hi):
    return _bf16_bits_high(hi) | lax.shift_right_logical(_bf16_bits_high(lo), 16)


def _pack_rows(x):
    parts = [_pack_bf16_pair(x[:, b:b + LANE], x[:, b + LANE:b + PACK_BLOCK])
             for b in range(0, x.shape[1], PACK_BLOCK)]
    return jnp.concatenate(parts, axis=1)


def _mod_kernel(c_ref, w_ref, b_ref, o_ref):
    a = _silu(c_ref[...])
    o_ref[...] = jnp.dot(a, w_ref[...], preferred_element_type=F32,
                         precision=lax.Precision.HIGHEST) + b_ref[...]


def _modulation(c_all, w_mod, b_mod):
    rows, d = c_all.shape
    n = w_mod.shape[1]
    return pl.pallas_call(
        _mod_kernel,
        grid=(n // d,),
        in_specs=[pl.BlockSpec((rows, d), lambda j: (0, 0)),
                  pl.BlockSpec((d, d), lambda j: (0, j)),
                  pl.BlockSpec((1, d), lambda j: (0, j))],
        out_specs=pl.BlockSpec((rows, d), lambda j: (0, j)),
        out_shape=jax.ShapeDtypeStruct((rows, n), F32),
        compiler_params=_cparams("arbitrary"),
        name="modulation",
    )(c_all, w_mod, b_mod)


def _ctx_kernel(raw_ref, ctx_ref, gain_ref, sh_ref, sc_ref, wkv_ref, sf_ref, sb_ref):
    x = ctx_ref[0]
    lc = x.shape[0]
    hc = _rms(x, gain_ref[...]) * (1.0 + sc_ref[...]) + sh_ref[...]
    kv = jnp.dot(hc.astype(BF16), wkv_ref[...], preferred_element_type=F32)
    pos = lax.broadcasted_iota(jnp.int32, (lc, RET_DK), 0).astype(F32)
    tn = (((0,), (0,)), ((), ()))
    for h in range(RET_HEADS):
        lgf = -_softplus(jnp.full((lc, RET_DK), raw_ref[0, h], F32))
        lgb = -_softplus(jnp.full((lc, RET_DK), raw_ref[1, h], F32))
        k = kv[:, h * RET_DK:(h + 1) * RET_DK] * (RET_DK ** -0.5)
        v = kv[:, (RET_HEADS + h) * RET_DK:(RET_HEADS + h + 1) * RET_DK].astype(BF16)
        kf = (k * jnp.exp(lgf * (lc - 1.0 - pos))).astype(BF16)
        kb = (k * jnp.exp(lgb * pos)).astype(BF16)
        sf_ref[0, h] = lax.dot_general(kf, v, tn, preferred_element_type=F32)
        sb_ref[0, h] = lax.dot_general(kb, v, tn, preferred_element_type=F32)


def _context_states(raw, ctx, gain0, csh, csc, w_kv):
    b, lc, d = ctx.shape
    st = jax.ShapeDtypeStruct((b, RET_HEADS, RET_DK, RET_DK), F32)
    st_spec = pl.BlockSpec((1, RET_HEADS, RET_DK, RET_DK), lambda i: (i, 0, 0, 0))
    vec = pl.BlockSpec((1, d), lambda i: (0, 0))
    return pl.pallas_call(
        _ctx_kernel,
        grid=(b,),
        in_specs=[pl.BlockSpec(memory_space=pltpu.SMEM),
                  pl.BlockSpec((1, lc, d), lambda i: (i, 0, 0)),
                  vec, vec, vec,
                  pl.BlockSpec(w_kv.shape, lambda i: (0, 0))],
        out_specs=[st_spec, st_spec],
        out_shape=[st, st],
        compiler_params=_cparams("arbitrary"),
        name="context_states",
    )(raw, ctx, gain0, csh, csc, w_kv)


IN_TM = 1024
IN_TILES = 8
IN_OUT_TILES = 7


def _inproj_kernel(x_ref, gain_ref, sh_ref, sc_ref, cos_ref, sin_ref, w_ref, o_ref, xn_ref):
    j = pl.program_id(1)

    @pl.when(j == 0)
    def _():
        h = _rms(x_ref[...], gain_ref[...]) * (1.0 + sc_ref[0]) + sh_ref[0]
        xn_ref[...] = h.astype(BF16)

    acc = jnp.dot(xn_ref[...], w_ref[...], preferred_element_type=F32)
    half = RET_DK // 2

    def rotary(scale):
        cos = cos_ref[...]
        sin = sin_ref[...]
        for h in range(RET_HEADS):
            t1 = acc[:, h * RET_DK:h * RET_DK + half]
            t2 = acc[:, h * RET_DK + half:(h + 1) * RET_DK]
            o_ref[:, h * RET_DK:h * RET_DK + half] = ((t1 * cos - t2 * sin) * scale).astype(BF16)
            o_ref[:, h * RET_DK + half:(h + 1) * RET_DK] = ((t1 * sin + t2 * cos) * scale).astype(BF16)

    @pl.when(j == 0)
    def _():
        rotary(1.0)

    @pl.when(j == 1)
    def _():
        rotary(RET_DK ** -0.5)

    @pl.when(j == 2)
    def _():
        o_ref[...] = acc.astype(BF16)

    @pl.when(j == 3)
    def _():
        o_ref[...] = _silu(acc).astype(BF16)

    hw = D_MODEL // 2

    @pl.when(j == 4)
    def _():
        o_ref[:, :hw] = (acc[:, :hw] * _sigmoid(acc[:, hw:])).astype(BF16)

    @pl.when(j == 5)
    def _():
        o_ref[:, hw:] = (acc[:, :hw] * _sigmoid(acc[:, hw:])).astype(BF16)

    @pl.when(j >= 6)
    def _():
        o_ref[...] = _sigmoid(acc).astype(BF16)


def _out_tile(j):
    return jnp.where(j <= 4, j, j - 1)


def _in_projection(x2, gain0, sh1, sc1, cos, sin, w_perm, seq):
    n, d = x2.shape
    tiles_per_seq = seq // IN_TM
    return pl.pallas_call(
        _inproj_kernel,
        grid=(n // IN_TM, IN_TILES),
        in_specs=[pl.BlockSpec((IN_TM, d), lambda i, j: (i, 0)),
                  pl.BlockSpec((1, d), lambda i, j: (0, 0)),
                  pl.BlockSpec((1, 1, d), lambda i, j: (i // tiles_per_seq, 0, 0)),
                  pl.BlockSpec((1, 1, d), lambda i, j: (i // tiles_per_seq, 0, 0)),
                  pl.BlockSpec((IN_TM, RET_DK // 2), lambda i, j: (i % tiles_per_seq, 0)),
                  pl.BlockSpec((IN_TM, RET_DK // 2), lambda i, j: (i % tiles_per_seq, 0)),
                  pl.BlockSpec((d, d), lambda i, j: (0, j))],
        out_specs=pl.BlockSpec((IN_TM, d), lambda i, j: (i, _out_tile(j))),
        out_shape=jax.ShapeDtypeStruct((n, IN_OUT_TILES * d), BF16),
        scratch_shapes=[pltpu.VMEM((IN_TM, d), BF16)],
        compiler_params=_cparams("arbitrary", "arbitrary"),
        name="in_projection",
    )(x2, gain0, sh1, sc1, cos, sin, w_perm)


def _ret_kernel(raw_ref, s0f_ref, s0b_ref, q_ref, k_ref, v_ref, g_ref, o_ref, sf_ref, sb_ref, y_ref):
    hd = pl.program_id(1)
    c = RET_CHUNK
    seq = q_ref.shape[0]
    nc = seq // c
    def log_gamma(direction, shape):
        return -_softplus(jnp.full(shape, raw_ref[direction, hd], F32))

    ri = lax.broadcasted_iota(jnp.int32, (c, c), 0).astype(F32)
    ci = lax.broadcasted_iota(jnp.int32, (c, c), 1).astype(F32)
    dec_f = jnp.where(ri >= ci, jnp.exp(log_gamma(0, (c, c)) * jnp.maximum(ri - ci, 0.0)), 0.0)
    dec_b = jnp.where(ci >= ri, jnp.exp(log_gamma(1, (c, c)) * jnp.maximum(ci - ri, 0.0)), 0.0)
    pos = lax.broadcasted_iota(jnp.int32, (c, RET_DK), 0).astype(F32)
    lgf = log_gamma(0, (c, RET_DK))
    lgb = log_gamma(1, (c, RET_DK))
    qdec_f = jnp.exp(lgf * (pos + 1.0))
    kdec_f = jnp.exp(lgf * (c - 1.0 - pos))
    qdec_b = jnp.exp(lgb * (c - pos))
    kdec_b = jnp.exp(lgb * pos)
    cdec_f = jnp.exp(log_gamma(0, (1, RET_DK)) * c)
    cdec_b = jnp.exp(log_gamma(1, (1, RET_DK)) * c)
    sf_ref[...] = s0f_ref[0, 0]
    sb_ref[...] = s0b_ref[0, 0]
    nt = (((1,), (1,)), ((), ()))
    tn = (((0,), (0,)), ((), ()))

    def chunk(st_ref, row, dec, qdec, kdec, cdec):
        q = q_ref[pl.ds(row, c), :]
        k = k_ref[pl.ds(row, c), :]
        v = v_ref[pl.ds(row, c), :]
        s = lax.dot_general(q, k, nt, preferred_element_type=F32) * dec
        st = st_ref[...]
        y = jnp.dot(s.astype(BF16), v, preferred_element_type=F32)
        y = y + jnp.dot((q.astype(F32) * qdec).astype(BF16), st.astype(BF16), preferred_element_type=F32)
        kd = (k.astype(F32) * kdec).astype(BF16)
        st_ref[...] = st * cdec + lax.dot_general(kd, v, tn, preferred_element_type=F32)
        return y

    def first_half(i, carry):
        rf = pl.multiple_of(i * c, c)
        rb = pl.multiple_of((nc - 1 - i) * c, c)
        y_ref[pl.ds(rf, c), :] = chunk(sf_ref, rf, dec_f, qdec_f, kdec_f, cdec_f)
        y_ref[pl.ds(rb, c), :] = chunk(sb_ref, rb, dec_b, qdec_b, kdec_b, cdec_b)
        return carry

    def second_half(i, carry):
        rf = pl.multiple_of(i * c, c)
        rb = pl.multiple_of((nc - 1 - i) * c, c)
        y_ref[pl.ds(rf, c), :] += chunk(sf_ref, rf, dec_f, qdec_f, kdec_f, cdec_f)
        y_ref[pl.ds(rb, c), :] += chunk(sb_ref, rb, dec_b, qdec_b, kdec_b, cdec_b)
        return carry

    lax.fori_loop(0, nc // 2, first_half, 0)
    lax.fori_loop(nc // 2, nc, second_half, 0)

    def finish(i, carry):
        r = pl.multiple_of(i * c, c)
        y = y_ref[pl.ds(r, c), :]
        yn = y * lax.rsqrt(jnp.mean(y * y, axis=-1, keepdims=True) + EPS)
        o_ref[pl.ds(r, c), :] = (yn * g_ref[pl.ds(r, c), :].astype(F32)).astype(BF16)
        return carry

    lax.fori_loop(0, nc, finish, 0)


def _retention(raw, st_f, st_b, proj, batch, seq):
    n = batch * seq
    st_spec = pl.BlockSpec((1, 1, RET_DK, RET_DK), lambda b, h: (b, h, 0, 0))

    def col(tile):
        return pl.BlockSpec((seq, RET_DK), lambda b, h: (b, tile * RET_HEADS + h))

    return pl.pallas_call(
        _ret_kernel,
        grid=(batch, RET_HEADS),
        in_specs=[pl.BlockSpec(memory_space=pltpu.SMEM), st_spec, st_spec,
                  col(0), col(1), col(2), col(3)],
        out_specs=pl.BlockSpec((seq, RET_DK), lambda b, h: (b, h)),
        out_shape=jax.ShapeDtypeStruct((n, RET_HEADS * RET_DK), BF16),
        scratch_shapes=[pltpu.VMEM((RET_DK, RET_DK), F32), pltpu.VMEM((RET_DK, RET_DK), F32),
                        pltpu.VMEM((seq, RET_DK), F32)],
        compiler_params=_cparams("arbitrary", "arbitrary"),
        name="retention",
    )(raw, st_f, st_b, proj, proj, proj, proj)


CONV_TL = 256
CONV_HALO = 16
CONV_RB = 32


def _conv_kernel(prev_ref, cur_ref, next_ref, cw_ref, cb_ref, lg_ref, lb_ref, o_ref, pad_ref, *, seq):
    i = pl.program_id(0)
    tl = CONV_TL
    has_prev = (i * tl) % seq != 0
    has_next = ((i + 1) * tl) % seq != 0
    pad_ref[0:CONV_HALO, :] = jnp.where(has_prev, prev_ref[...].astype(F32), 0.0)
    pad_ref[CONV_HALO:CONV_HALO + tl, :] = cur_ref[...].astype(F32)
    pad_ref[CONV_HALO + tl:, :] = jnp.where(has_next, next_ref[...].astype(F32), 0.0)
    off = CONV_HALO - CONV_HALF
    for rb in range(tl // CONV_RB):
        base = rb * CONV_RB + off
        acc = pad_ref[base:base + CONV_RB, :] * cw_ref[0:1, :]
        for w in range(1, CONV_WIDTH):
            acc = acc + pad_ref[base + w:base + w + CONV_RB, :] * cw_ref[w:w + 1, :]
        u = acc + cb_ref[...]
        mu = jnp.mean(u, axis=-1, keepdims=True)
        uc = u - mu
        y = uc * lax.rsqrt(jnp.mean(uc * uc, axis=-1, keepdims=True) + EPS) * lg_ref[...] + lb_ref[...]
        o_ref[rb * CONV_RB:(rb + 1) * CONV_RB, :] = _silu(y).astype(BF16)


def _conv_branch(proj, conv_w, conv_b, ln_g, ln_b, seq):
    n = proj.shape[0]
    d = D_MODEL
    glu_tile = 4
    hb = CONV_TL // CONV_HALO
    n_halo = n // CONV_HALO
    vec = pl.BlockSpec((1, d), lambda i: (0, 0))
    return pl.pallas_call(
        functools.partial(_conv_kernel, seq=seq),
        grid=(n // CONV_TL,),
        in_specs=[pl.BlockSpec((CONV_HALO, d), lambda i: (jnp.maximum(i * hb - 1, 0), glu_tile)),
                  pl.BlockSpec((CONV_TL, d), lambda i: (i, glu_tile)),
                  pl.BlockSpec((CONV_HALO, d), lambda i: (jnp.minimum((i + 1) * hb, n_halo - 1), glu_tile)),
                  pl.BlockSpec((CONV_WIDTH, d), lambda i: (0, 0)),
                  vec, vec, vec],
        out_specs=pl.BlockSpec((CONV_TL, d), lambda i: (i, 0)),
        out_shape=jax.ShapeDtypeStruct((n, d), BF16),
        scratch_shapes=[pltpu.VMEM((CONV_TL + 2 * CONV_HALO, d), F32)],
        compiler_params=_cparams("arbitrary"),
        name="conv_branch",
    )(proj, proj, proj, conv_w, conv_b, ln_g, ln_b)


MERGE_TM = 256


def _merge_kernel(x_ref, yr_ref, uc_ref, ga_ref, gb_ref, wr_ref, wc_ref, wo_ref, wq_ref,
                  gain1_ref, gain2_ref, g1_ref, sh2_ref, sc2_ref, lat_ref, h2_ref, qp_ref):
    a = jnp.dot(yr_ref[...], wr_ref[...], preferred_element_type=F32)
    b = jnp.dot(uc_ref[...], wc_ref[...], preferred_element_type=F32)
    merged = ga_ref[...].astype(F32) * a + gb_ref[...].astype(F32) * b
    y = jnp.dot(merged.astype(BF16), wo_ref[...], preferred_element_type=F32)
    lat = x_ref[...] + g1_ref[0] * _rms(y, gain1_ref[...])
    lat_ref[...] = lat
    h2 = _rms(lat, gain2_ref[...]) * (1.0 + sc2_ref[0]) + sh2_ref[0]
    h2_ref[...] = _pack_rows(h2)
    qp_ref[...] = jnp.dot(h2.astype(BF16), wq_ref[...], preferred_element_type=F32).astype(BF16)


def _merge(x2, y_ret, u_conv, proj, w_ret_o, w_conv_o, w_out, w_q, gain1, gain2, g1, sh2, sc2, seq):
    n, d = x2.shape
    tps = seq // MERGE_TM
    nq = w_q.shape[1]
    row = pl.BlockSpec((MERGE_TM, d), lambda i: (i, 0))
    wsp = pl.BlockSpec((d, d), lambda i: (0, 0))
    vec = pl.BlockSpec((1, d), lambda i: (0, 0))
    bvec = pl.BlockSpec((1, 1, d), lambda i: (i // tps, 0, 0))
    return pl.pallas_call(
        _merge_kernel,
        grid=(n // MERGE_TM,),
        in_specs=[row, row, row,
                  pl.BlockSpec((MERGE_TM, d), lambda i: (i, 5)),
                  pl.BlockSpec((MERGE_TM, d), lambda i: (i, 6)),
                  wsp, wsp, wsp,
                  pl.BlockSpec((d, nq), lambda i: (0, 0)),
                  vec, vec, bvec, bvec, bvec],
        out_specs=[row, pl.BlockSpec((MERGE_TM, d // 2), lambda i: (i, 0)),
                   pl.BlockSpec((MERGE_TM, nq), lambda i: (i, 0))],
        out_shape=[jax.ShapeDtypeStruct((n, d), F32), jax.ShapeDtypeStruct((n, d // 2), jnp.int32),
                   jax.ShapeDtypeStruct((n, nq), BF16)],
        compiler_params=_cparams("arbitrary"),
        name="merge",
    )(x2, y_ret, u_conv, proj, proj, w_ret_o, w_conv_o, w_out, w_q, gain1, gain2, g1, sh2, sc2)


ROUTE_T = 512


def _topk_rows(s, k, payload=None):
    r = s.shape[0]
    rows = lax.broadcasted_iota(jnp.int32, s.shape, 0).astype(F32)
    vals, idxs = [], []
    for _ in range(k):
        m = jnp.max(s, axis=0, keepdims=True)
        pos = jnp.min(jnp.where(s == m, rows, float(r)), axis=0, keepdims=True)
        hit = rows == pos
        vals.append(m)
        if payload is None:
            idxs.append(pos)
        else:
            idxs.append(jnp.max(jnp.where(hit, payload, -1.0), axis=0, keepdims=True))
        s = jnp.where(hit, -jnp.inf, s)
    return jnp.concatenate(vals, axis=0), jnp.concatenate(idxs, axis=0)


def _route_kernel(q_ref, keys_ref, e_ref, g_ref, es_ref, gs_ref):
    h = pl.program_id(1)
    nt = (((1,), (1,)), ((), ()))
    row0 = pl.multiple_of(h * PEER_TOPK, PEER_TOPK)
    for t in range(ROUTE_T // LANE):
        cols = slice(t * LANE, (t + 1) * LANE)
        tops = []
        for p in range(2):
            qh = q_ref[cols, p * PEER_DK_HALF:(p + 1) * PEER_DK_HALF]
            s = lax.dot_general(keys_ref[0, p], qh, nt, preferred_element_type=F32)
            tops.append(_topk_rows(s, PEER_TOPK))
        (v1, i1), (v2, i2) = tops
        widths = [PEER_TOPK // (a + 1) for a in range(PEER_TOPK)]
        pad = (-sum(widths)) % 8
        cand = jnp.concatenate([v1[a:a + 1] + v2[:w] for a, w in enumerate(widths)]
                               + [jnp.full((pad, LANE), -jnp.inf, F32)], axis=0)
        cidx = jnp.concatenate([i1[a:a + 1] * float(PEER_N_KEYS) + i2[:w] for a, w in enumerate(widths)]
                               + [jnp.zeros((pad, LANE), F32)], axis=0)
        best, experts = _topk_rows(cand, PEER_TOPK, payload=cidx)
        ex = jnp.exp(best - best[0:1])
        gates = ex / jnp.sum(ex, axis=0, keepdims=True)
        es_ref[pl.ds(row0, PEER_TOPK), cols] = experts.astype(jnp.int32)
        gs_ref[pl.ds(row0, PEER_TOPK), cols] = gates

    @pl.when(h == PEER_HEADS - 1)
    def _():
        first_piece = es_ref[...].T * PIECES
        for s in range(PIECES):
            e_ref[:, s * PEER_SEL:(s + 1) * PEER_SEL] = first_piece + s
        g_ref[...] = gs_ref[...].T


def _route(qp, keys):
    n = qp.shape[0]
    hw = 2 * PEER_DK_HALF
    out = pl.BlockSpec((ROUTE_T, PEER_SEL), lambda i, h: (i, 0))
    return pl.pallas_call(
        _route_kernel,
        grid=(n // ROUTE_T, PEER_HEADS),
        in_specs=[pl.BlockSpec((ROUTE_T, hw), lambda i, h: (i, h)),
                  pl.BlockSpec((1, 2, PEER_N_KEYS, PEER_DK_HALF), lambda i, h: (h, 0, 0, 0))],
        out_specs=[pl.BlockSpec((ROUTE_T, PIECES * PEER_SEL), lambda i, h: (i, 0)), out],
        out_shape=[jax.ShapeDtypeStruct((n, PIECES * PEER_SEL), jnp.int32),
                   jax.ShapeDtypeStruct((n, PEER_SEL), F32)],
        scratch_shapes=[pltpu.VMEM((PEER_SEL, ROUTE_T), jnp.int32), pltpu.VMEM((PEER_SEL, ROUTE_T), F32)],
        compiler_params=_cparams("arbitrary", "arbitrary"),
        name="peer_route",
    )(qp, keys)


def _sc_mesh():
    return plsc.VectorSubcoreMesh(core_axis_name="c", subcore_axis_name="s")


def _worker_id():
    return lax.axis_index("s") * SC_CORES + lax.axis_index("c")


SC_NBUF = PEER_SEL // SC_ROWS
SC_QUAD = 4
SC_GROUP = 8
SC_WORDS = LANE // SC_LANES


def _sc_scratch():
    assert SC_NBUF == PIECES
    return [pltpu.VMEM((SC_NBUF, PIECES, SC_ROWS, LANE), jnp.int32)] + [pltpu.SemaphoreType.DMA] * SC_NBUF


def _gather(tab_hbm, idx_v, rows, sem, tl, ch, s):
    return pltpu.make_async_copy(tab_hbm.at[idx_v.at[tl, pl.ds(s * PEER_SEL + ch * SC_ROWS, SC_ROWS)]],
                                 rows.at[ch, s], sem)


def _widen_pair(packed_bf16):
    words = plsc.bitcast(packed_bf16, jnp.int32)
    return (plsc.bitcast(lax.shift_left(words, 16), F32), plsc.bitcast(words & HIGH_HALF, F32))


def _peer_scores_sc(pieces, xw, piece_idx):
    n = xw.shape[0]
    per = n // SC_WORKERS

    @functools.partial(
        pl.kernel, mesh=_sc_mesh(),
        out_type=jax.ShapeDtypeStruct((n, PEER_SEL), F32),
        scratch_types=[pltpu.VMEM((SC_TB, PIECES * PEER_SEL), jnp.int32), pltpu.VMEM((SC_TB, PIECES * LANE), jnp.int32),
                       pltpu.VMEM((SC_TB, PEER_SEL), F32), pltpu.VMEM((SC_ROWS, SC_LANES), F32)] + _sc_scratch(),
        compiler_params=pltpu.CompilerParams(needs_layout_passes=False),
        name="peer_scores_sc",
    )
    def run(tab_hbm, x_hbm, idx_hbm, out_hbm, idx_v, x_v, res_v, part_v, rows_all, *sems):
        base = _worker_id() * per
        lane = lax.iota(jnp.int32, SC_LANES)
        zero = jnp.zeros((SC_LANES,), F32)

        def gathers(tl, ch):
            return [_gather(tab_hbm, idx_v, rows_all, sems[ch], tl, ch, s) for s in range(PIECES)]

        @pl.loop(0, per // SC_TB)
        def _(blk):
            tok0 = pl.multiple_of(base + blk * SC_TB, SC_TB)
            pltpu.sync_copy(idx_hbm.at[pl.ds(tok0, SC_TB)], idx_v)
            pltpu.sync_copy(x_hbm.at[pl.ds(tok0, SC_TB)], x_v)
            for ch in range(SC_NBUF):
                for cp in gathers(0, ch):
                    cp.start()

            @pl.loop(0, SC_TB)
            def _(tl):
                for ch in range(SC_NBUF):
                    for cp in gathers(tl, ch):
                        cp.wait()
                    rows = rows_all.at[ch]

                    @pl.loop(0, SC_ROWS // SC_GROUP)
                    def _(g):
                        r0 = pl.multiple_of(g * SC_GROUP, SC_GROUP)
                        accs = [zero] * SC_GROUP
                        for s in range(PIECES):
                            for w0 in range(0, LANE, SC_QUAD * SC_LANES):
                                xs = [plsc.bitcast(x_v[tl, pl.ds(s * LANE + w0 + j * SC_LANES, SC_LANES)], BF16)
                                      for j in range(SC_QUAD)]
                                for r in range(SC_GROUP):
                                    part = None
                                    for j in range(SC_QUAD):
                                        prod = plsc.bitcast(rows[s, r0 + r, pl.ds(w0 + j * SC_LANES, SC_LANES)],
                                                            BF16) * xs[j]
                                        part = prod if part is None else part + prod
                                    lo, hi = _widen_pair(part)
                                    accs[r] = accs[r] + lo + hi
                        for r in range(SC_GROUP):
                            part_v[r0 + r, :] = accs[r]
                    for g in range(SC_ROWS // SC_LANES):
                        row_ids = lane + g * SC_LANES
                        res = zero
                        for col in range(SC_LANES):
                            res = res + plsc.load_gather(part_v, [row_ids, jnp.full((SC_LANES,), col, jnp.int32)])
                        res_v[tl, pl.ds(ch * SC_ROWS + g * SC_LANES, SC_LANES)] = res

                    @pl.when(tl + 1 < SC_TB)
                    def _():
                        for cp in gathers(tl + 1, ch):
                            cp.start()

            pltpu.sync_copy(res_v, out_hbm.at[pl.ds(tok0, SC_TB)])

    return run(pieces, xw, piece_idx)


def _peer_combine_sc(pieces, weight_words, piece_idx):
    n = weight_words.shape[0]
    per = n // SC_WORKERS
    d = PIECES * PACK_BLOCK

    @functools.partial(
        pl.kernel, mesh=_sc_mesh(),
        out_type=jax.ShapeDtypeStruct((n, d), F32),
        scratch_types=[pltpu.VMEM((SC_TB, PIECES * PEER_SEL), jnp.int32), pltpu.VMEM((SC_TB, PEER_SEL), jnp.int32),
                       pltpu.VMEM((SC_TB, d), F32)] + _sc_scratch(),
        compiler_params=pltpu.CompilerParams(needs_layout_passes=False),
        name="peer_combine_sc",
    )
    def run(tab_hbm, w_hbm, idx_hbm, out_hbm, idx_v, w_v, acc_v, rows_all, *sems):
        base = _worker_id() * per
        zero = jnp.zeros((SC_LANES,), F32)
        quads_per_chunk = SC_ROWS // SC_QUAD

        def gathers(tl, s):
            return [_gather(tab_hbm, idx_v, rows_all, sems[s], tl, ch, s) for ch in range(SC_NBUF)]

        @pl.loop(0, per // SC_TB)
        def _(blk):
            tok0 = pl.multiple_of(base + blk * SC_TB, SC_TB)
            pltpu.sync_copy(idx_hbm.at[pl.ds(tok0, SC_TB)], idx_v)
            pltpu.sync_copy(w_hbm.at[pl.ds(tok0, SC_TB)], w_v)
            for s in range(PIECES):
                for cp in gathers(0, s):
                    cp.start()

            @pl.loop(0, SC_TB)
            def _(tl):
                tl_vec = jnp.full((SC_LANES,), 0, jnp.int32) + tl
                for s in range(PIECES):
                    for cp in gathers(tl, s):
                        cp.wait()

                    def row_quad(q, accs):
                        ch = lax.shift_right_logical(q, quads_per_chunk.bit_length() - 1)
                        r0 = pl.multiple_of((q & (quads_per_chunk - 1)) * SC_QUAD, SC_QUAD)
                        quad = rows_all.at[ch, s, pl.ds(r0, SC_QUAD)]
                        col0 = jnp.full((SC_LANES,), 0, jnp.int32) + q * SC_QUAD
                        ws = [plsc.bitcast(plsc.load_gather(w_v, [tl_vec, col0 + j]), BF16) for j in range(SC_QUAD)]
                        accs = list(accs)
                        for k in range(SC_WORDS):
                            part = None
                            for j in range(SC_QUAD):
                                prod = plsc.bitcast(quad[j, pl.ds(k * SC_LANES, SC_LANES)], BF16) * ws[j]
                                part = prod if part is None else part + prod
                            lo, hi = _widen_pair(part)
                            accs[k] = accs[k] + lo
                            accs[SC_WORDS + k] = accs[SC_WORDS + k] + hi
                        return tuple(accs)

                    accs = lax.fori_loop(0, PEER_SEL // SC_QUAD, row_quad, (zero,) * (2 * SC_WORDS))
                    for i, acc in enumerate(accs):
                        acc_v[tl, pl.ds(s * PACK_BLOCK + i * SC_LANES, SC_LANES)] = acc

                    @pl.when(tl + 1 < SC_TB)
                    def _():
                        for cp in gathers(tl + 1, s):
                            cp.start()

            pltpu.sync_copy(acc_v, out_hbm.at[pl.ds(tok0, SC_TB)])

    return run(pieces, weight_words, piece_idx)


EW_TM = 2048


def _gelu_tanh(x):
    c = 0.7978845608028654
    return 0.5 * x * (1.0 + jnp.tanh(c * (x + 0.044715 * (x * x * x))))


def _gate_kernel(s_ref, g_ref, o_ref):
    w = g_ref[...] * _gelu_tanh(s_ref[...])
    o_ref[...] = _pack_bf16_pair(w, w)


def _gate_weights(scores, gates):
    n, k = scores.shape
    spec = pl.BlockSpec((EW_TM, k), lambda i: (i, 0))
    return pl.pallas_call(
        _gate_kernel, grid=(n // EW_TM,), in_specs=[spec, spec], out_specs=spec,
        out_shape=jax.ShapeDtypeStruct((n, k), jnp.int32),
        compiler_params=_cparams("arbitrary"), name="peer_gate",
    )(scores, gates)


FIN_TM = 1024


def _final_kernel(lat_ref, p_ref, gain_ref, g2_ref, o_ref):
    o_ref[...] = lat_ref[...] + g2_ref[0] * _rms(p_ref[...], gain_ref[...])


def _final(lat, peer, gain3, g2, seq):
    n, d = lat.shape
    tps = seq // FIN_TM
    row = pl.BlockSpec((FIN_TM, d), lambda i: (i, 0))
    return pl.pallas_call(
        _final_kernel, grid=(n // FIN_TM,),
        in_specs=[row, row, pl.BlockSpec((1, d), lambda i: (0, 0)),
                  pl.BlockSpec((1, 1, d), lambda i: (i // tps, 0, 0))],
        out_specs=row, out_shape=jax.ShapeDtypeStruct((n, d), F32),
        compiler_params=_cparams("arbitrary"), name="final_residual",
    )(lat, peer, gain3, g2)


def _expert_pieces(table):
    e, d = table.shape
    bits = lax.bitcast_convert_type(table.astype(BF16), jnp.uint16).astype(jnp.uint32)
    bits = bits.reshape(e, d // PACK_BLOCK, 2, LANE)
    words = bits[:, :, 0, :] | (bits[:, :, 1, :] << 16)
    return lax.bitcast_convert_type(words, jnp.int32).reshape(e * (d // PACK_BLOCK), LANE)


def _axial_rotary(seq):
    n_rows = seq // GRID_W
    rows = jnp.repeat(jnp.arange(n_rows, dtype=F32), GRID_W)
    cols = jnp.tile(jnp.arange(GRID_W, dtype=F32), n_rows)
    quarter = RET_DK // 4
    inv = ROPE_BASE ** (-jnp.arange(quarter, dtype=F32) / quarter)
    ang = jnp.concatenate([rows[:, None] * inv, cols[:, None] * inv], axis=-1)
    return jnp.cos(ang), jnp.sin(ang)


def kernel(x, c, ctx, c_ctx, w_mod, b_mod, norm_gain, w_in, ret_decay_raw, w_ret_o, conv_w, conv_b,
           conv_norm_g, conv_norm_b, w_conv_o, w_out, peer_wq, peer_keys, peer_u, peer_v):
    assert w_mod.shape[0] == 1, "single layer"
    batch, seq, d = x.shape
    n = batch * seq
    assert d == D_MODEL and seq % IN_TM == 0 and seq % (SC_WORKERS * SC_TB) == 0
    gain = norm_gain[0]
    x2 = x.reshape(n, d)

    pad = (-(batch + 1)) % 8
    c_all = jnp.concatenate([c, c_ctx[None, :], jnp.zeros((pad, d), F32)], axis=0)
    mod = _modulation(c_all, w_mod[0], b_mod[0][None, :])
    mod_lat = mod[:batch].reshape(batch, N_MOD, 1, d)
    sh1, sc1, g1, sh2, sc2, g2 = (mod_lat[:, i] for i in range(N_MOD))
    mod_ctx = mod[batch:batch + 1].reshape(N_MOD, 1, d)

    w_in0 = w_in[0]
    q_w = RET_HEADS * RET_DK
    w_kv = w_in0[:, q_w:3 * q_w].astype(BF16)
    st_f, st_b = _context_states(ret_decay_raw[0], ctx, gain[0:1], mod_ctx[0], mod_ctx[1], w_kv)

    hw = d // 2
    glu_a = w_in0[:, 4 * d:5 * d]
    glu_b = w_in0[:, 5 * d:6 * d]
    w_perm = jnp.concatenate([w_in0[:, :4 * d], glu_a[:, :hw], glu_b[:, :hw], glu_a[:, hw:], glu_b[:, hw:],
                              w_in0[:, 6 * d:]], axis=1).astype(BF16)
    cos, sin = _axial_rotary(seq)
    w_ret_b, w_conv_b, w_out_b, w_q_b = (w[0].astype(BF16) for w in (w_ret_o, w_conv_o, w_out, peer_wq))
    keys_b = peer_keys[0].astype(BF16)
    u_pieces = _expert_pieces(peer_u[0])
    v_pieces = _expert_pieces(peer_v[0])
    conv_vecs = (conv_b[0][None, :], conv_norm_g[0][None, :], conv_norm_b[0][None, :])

    groups = BATCH_GROUPS if batch % BATCH_GROUPS == 0 else 1
    gb = batch // groups
    for tile_rows in (IN_TM, CONV_TL, MERGE_TM, ROUTE_T, EW_TM, FIN_TM, SC_WORKERS * SC_TB):
        assert (gb * seq) % tile_rows == 0, "token count per batch group must be a multiple of every row tile"
    outs = []
    for g in range(groups):
        bs = slice(g * gb, (g + 1) * gb)
        xg = x2[g * gb * seq:(g + 1) * gb * seq]
        proj = _in_projection(xg, gain[0:1], sh1[bs], sc1[bs], cos, sin, w_perm, seq)
        y_ret = _retention(ret_decay_raw[0], st_f[bs], st_b[bs], proj, gb, seq)
        u_conv = _conv_branch(proj, conv_w[0], *conv_vecs, seq)
        lat1, h2_words, qp = _merge(xg, y_ret, u_conv, proj, w_ret_b, w_conv_b, w_out_b, w_q_b, gain[1:2],
                                    gain[2:3], g1[bs], sh2[bs], sc2[bs], seq)
        piece_idx, gates = _route(qp, keys_b)
        scores = _peer_scores_sc(u_pieces, h2_words, piece_idx)
        weight_words = _gate_weights(scores, gates)
        peer_out = _peer_combine_sc(v_pieces, weight_words, piece_idx)
        outs.append(_final(lat1, peer_out, gain[3:4], g2[bs], seq))
    return jnp.concatenate(outs, axis=0).reshape(batch, seq, d)
```

```python
import functools

import jax
import jax.numpy as jnp
from jax import lax
from jax.experimental import pallas as pl
from jax.experimental.pallas import tpu as pltpu
from jax.experimental.pallas import tpu_sc as plsc

F32 = jnp.float32
BF16 = jnp.bfloat16

D_MODEL = 1024
GRID_W = 64
EPS = 1e-6
N_MOD = 6
RET_HEADS = 4
RET_DK = 256
RET_CHUNK = 128
ROPE_BASE = 10000.0
CONV_WIDTH = 31
CONV_HALF = CONV_WIDTH // 2
PEER_HEADS = 8
PEER_N_KEYS = 128
PEER_DK_HALF = 128
PEER_TOPK = 16
PEER_SEL = PEER_HEADS * PEER_TOPK

SC_CORES = 2
SC_SUBCORES = 16
SC_WORKERS = SC_CORES * SC_SUBCORES
SC_LANES = 16
SC_ROWS = 32
SC_TB = 16
LANE = 128
PACK_BLOCK = 2 * LANE
PIECES = D_MODEL // PACK_BLOCK
HIGH_HALF = -65536

BATCH_GROUPS = (1, 2, 3, 4, 6)

VMEM_LIMIT = 48 * 1024 * 1024


def _cparams(*sem):
    return pltpu.CompilerParams(dimension_semantics=sem, vmem_limit_bytes=VMEM_LIMIT)


def _sigmoid(x):
    return 1.0 / (1.0 + jnp.exp(-x))


def _silu(x):
    return x * _sigmoid(x)


def _softplus(x):
    return jnp.maximum(x, 0.0) + jnp.log1p(jnp.exp(-jnp.abs(x)))


def _rms(x, gain):
    return x * lax.rsqrt(jnp.mean(x * x, axis=-1, keepdims=True) + EPS) * gain


def _bf16_bits_high(x):
    return lax.bitcast_convert_type(x.astype(BF16).astype(F32), jnp.int32)


def _pack_bf16_pair(lo, hi):
    return _bf16_bits_high(hi) | lax.shift_right_logical(_bf16_bits_high(lo), 16)


def _pack_rows(x):
    parts = [_pack_bf16_pair(x[:, b:b + LANE], x[:, b + LANE:b + PACK_BLOCK])
             for b in range(0, x.shape[1], PACK_BLOCK)]
    return jnp.concatenate(parts, axis=1)


def _mod_kernel(c_ref, w_ref, b_ref, o_ref):
    a = _silu(c_ref[...])
    o_ref[...] = jnp.dot(a, w_ref[...], preferred_element_type=F32,
                         precision=lax.Precision.HIGHEST) + b_ref[...]


def _modulation(c_all, w_mod, b_mod):
    rows, d = c_all.shape
    n = w_mod.shape[1]
    return pl.pallas_call(
        _mod_kernel,
        grid=(n // d,),
        in_specs=[pl.BlockSpec((rows, d), lambda j: (0, 0)),
                  pl.BlockSpec((d, d), lambda j: (0, j)),
                  pl.BlockSpec((1, d), lambda j: (0, j))],
        out_specs=pl.BlockSpec((rows, d), lambda j: (0, j)),
        out_shape=jax.ShapeDtypeStruct((rows, n), F32),
        compiler_params=_cparams("arbitrary"),
        name="modulation",
    )(c_all, w_mod, b_mod)


def _ctx_kernel(raw_ref, ctx_ref, gain_ref, sh_ref, sc_ref, wkv_ref, sf_ref, sb_ref):
    x = ctx_ref[0]
    lc = x.shape[0]
    hc = _rms(x, gain_ref[...]) * (1.0 + sc_ref[...]) + sh_ref[...]
    kv = jnp.dot(hc.astype(BF16), wkv_ref[...], preferred_element_type=F32)
    pos = lax.broadcasted_iota(jnp.int32, (lc, RET_DK), 0).astype(F32)
    tn = (((0,), (0,)), ((), ()))
    for h in range(RET_HEADS):
        lgf = -_softplus(jnp.full((lc, RET_DK), raw_ref[0, h], F32))
        lgb = -_softplus(jnp.full((lc, RET_DK), raw_ref[1, h], F32))
        k = kv[:, h * RET_DK:(h + 1) * RET_DK] * (RET_DK ** -0.5)
        v = kv[:, (RET_HEADS + h) * RET_DK:(RET_HEADS + h + 1) * RET_DK].astype(BF16)
        kf = (k * jnp.exp(lgf * (lc - 1.0 - pos))).astype(BF16)
        kb = (k * jnp.exp(lgb * pos)).astype(BF16)
        sf_ref[0, h] = lax.dot_general(kf, v, tn, preferred_element_type=F32)
        sb_ref[0, h] = lax.dot_general(kb, v, tn, preferred_element_type=F32)


def _context_states(raw, ctx, gain0, csh, csc, w_kv):
    b, lc, d = ctx.shape
    st = jax.ShapeDtypeStruct((b, RET_HEADS, RET_DK, RET_DK), F32)
    st_spec = pl.BlockSpec((1, RET_HEADS, RET_DK, RET_DK), lambda i: (i, 0, 0, 0))
    vec = pl.BlockSpec((1, d), lambda i: (0, 0))
    return pl.pallas_call(
        _ctx_kernel,
        grid=(b,),
        in_specs=[pl.BlockSpec(memory_space=pltpu.SMEM),
                  pl.BlockSpec((1, lc, d), lambda i: (i, 0, 0)),
                  vec, vec, vec,
                  pl.BlockSpec(w_kv.shape, lambda i: (0, 0))],
        out_specs=[st_spec, st_spec],
        out_shape=[st, st],
        compiler_params=_cparams("arbitrary"),
        name="context_states",
    )(raw, ctx, gain0, csh, csc, w_kv)


IN_TM = 1024
IN_TILES = 8
IN_OUT_TILES = 7


def _inproj_kernel(x_ref, gain_ref, sh_ref, sc_ref, cos_ref, sin_ref, w_ref, after_ref, o_ref, xn_ref):
    del after_ref
    j = pl.program_id(1)

    @pl.when(j == 0)
    def _():
        h = _rms(x_ref[...], gain_ref[...]) * (1.0 + sc_ref[0]) + sh_ref[0]
        xn_ref[...] = h.astype(BF16)

    acc = jnp.dot(xn_ref[...], w_ref[...], preferred_element_type=F32)
    half = RET_DK // 2

    def rotary(scale):
        cos = cos_ref[...]
        sin = sin_ref[...]
        for h in range(RET_HEADS):
            t1 = acc[:, h * RET_DK:h * RET_DK + half]
            t2 = acc[:, h * RET_DK + half:(h + 1) * RET_DK]
            o_ref[:, h * RET_DK:h * RET_DK + half] = ((t1 * cos - t2 * sin) * scale).astype(BF16)
            o_ref[:, h * RET_DK + half:(h + 1) * RET_DK] = ((t1 * sin + t2 * cos) * scale).astype(BF16)

    @pl.when(j == 0)
    def _():
        rotary(1.0)

    @pl.when(j == 1)
    def _():
        rotary(RET_DK ** -0.5)

    @pl.when(j == 2)
    def _():
        o_ref[...] = acc.astype(BF16)

    @pl.when(j == 3)
    def _():
        o_ref[...] = _silu(acc).astype(BF16)

    hw = D_MODEL // 2

    @pl.when(j == 4)
    def _():
        o_ref[:, :hw] = (acc[:, :hw] * _sigmoid(acc[:, hw:])).astype(BF16)

    @pl.when(j == 5)
    def _():
        o_ref[:, hw:] = (acc[:, :hw] * _sigmoid(acc[:, hw:])).astype(BF16)

    @pl.when(j >= 6)
    def _():
        o_ref[...] = _sigmoid(acc).astype(BF16)


def _out_tile(j):
    return jnp.where(j <= 4, j, j - 1)


def _in_projection(x2, gain0, sh1, sc1, cos, sin, w_perm, after, seq):
    n, d = x2.shape
    tiles_per_seq = seq // IN_TM
    return pl.pallas_call(
        _inproj_kernel,
        grid=(n // IN_TM, IN_TILES),
        in_specs=[pl.BlockSpec((IN_TM, d), lambda i, j: (i, 0)),
                  pl.BlockSpec((1, d), lambda i, j: (0, 0)),
                  pl.BlockSpec((1, 1, d), lambda i, j: (i // tiles_per_seq, 0, 0)),
                  pl.BlockSpec((1, 1, d), lambda i, j: (i // tiles_per_seq, 0, 0)),
                  pl.BlockSpec((IN_TM, RET_DK // 2), lambda i, j: (i % tiles_per_seq, 0)),
                  pl.BlockSpec((IN_TM, RET_DK // 2), lambda i, j: (i % tiles_per_seq, 0)),
                  pl.BlockSpec((d, d), lambda i, j: (0, j)),
                  pl.BlockSpec(after.shape, lambda i, j: (0, 0))],
        out_specs=pl.BlockSpec((IN_TM, d), lambda i, j: (i, _out_tile(j))),
        out_shape=jax.ShapeDtypeStruct((n, IN_OUT_TILES * d), BF16),
        scratch_shapes=[pltpu.VMEM((IN_TM, d), BF16)],
        compiler_params=_cparams("arbitrary", "arbitrary"),
        name="in_projection",
    )(x2, gain0, sh1, sc1, cos, sin, w_perm, after)


def _ret_kernel(raw_ref, s0f_ref, s0b_ref, q_ref, k_ref, v_ref, g_ref, o_ref, sf_ref, sb_ref, y_ref):
    hd = pl.program_id(1)
    c = RET_CHUNK
    seq = q_ref.shape[0]
    nc = seq // c
    def log_gamma(direction, shape):
        return -_softplus(jnp.full(shape, raw_ref[direction, hd], F32))

    ri = lax.broadcasted_iota(jnp.int32, (c, c), 0).astype(F32)
    ci = lax.broadcasted_iota(jnp.int32, (c, c), 1).astype(F32)
    dec_f = jnp.where(ri >= ci, jnp.exp(log_gamma(0, (c, c)) * jnp.maximum(ri - ci, 0.0)), 0.0)
    dec_b = jnp.where(ci >= ri, jnp.exp(log_gamma(1, (c, c)) * jnp.maximum(ci - ri, 0.0)), 0.0)
    pos = lax.broadcasted_iota(jnp.int32, (c, RET_DK), 0).astype(F32)
    lgf = log_gamma(0, (c, RET_DK))
    lgb = log_gamma(1, (c, RET_DK))
    qdec_f = jnp.exp(lgf * (pos + 1.0))
    kdec_f = jnp.exp(lgf * (c - 1.0 - pos))
    qdec_b = jnp.exp(lgb * (c - pos))
    kdec_b = jnp.exp(lgb * pos)
    cdec_f = jnp.exp(log_gamma(0, (1, RET_DK)) * c)
    cdec_b = jnp.exp(log_gamma(1, (1, RET_DK)) * c)
    sf_ref[...] = s0f_ref[0, 0]
    sb_ref[...] = s0b_ref[0, 0]
    nt = (((1,), (1,)), ((), ()))
    tn = (((0,), (0,)), ((), ()))

    def chunk(st_ref, row, dec, qdec, kdec, cdec):
        q = q_ref[pl.ds(row, c), :]
        k = k_ref[pl.ds(row, c), :]
        v = v_ref[pl.ds(row, c), :]
        s = lax.dot_general(q, k, nt, preferred_element_type=F32) * dec
        st = st_ref[...]
        y = jnp.dot(s.astype(BF16), v, preferred_element_type=F32)
        y = y + jnp.dot((q.astype(F32) * qdec).astype(BF16), st.astype(BF16), preferred_element_type=F32)
        kd = (k.astype(F32) * kdec).astype(BF16)
        st_ref[...] = st * cdec + lax.dot_general(kd, v, tn, preferred_element_type=F32)
        return y

    def first_half(i, carry):
        rf = pl.multiple_of(i * c, c)
        rb = pl.multiple_of((nc - 1 - i) * c, c)
        y_ref[pl.ds(rf, c), :] = chunk(sf_ref, rf, dec_f, qdec_f, kdec_f, cdec_f)
        y_ref[pl.ds(rb, c), :] = chunk(sb_ref, rb, dec_b, qdec_b, kdec_b, cdec_b)
        return carry

    def second_half(i, carry):
        rf = pl.multiple_of(i * c, c)
        rb = pl.multiple_of((nc - 1 - i) * c, c)
        y_ref[pl.ds(rf, c), :] += chunk(sf_ref, rf, dec_f, qdec_f, kdec_f, cdec_f)
        y_ref[pl.ds(rb, c), :] += chunk(sb_ref, rb, dec_b, qdec_b, kdec_b, cdec_b)
        return carry

    lax.fori_loop(0, nc // 2, first_half, 0)
    lax.fori_loop(nc // 2, nc, second_half, 0)

    def finish(i, carry):
        r = pl.multiple_of(i * c, c)
        y = y_ref[pl.ds(r, c), :]
        yn = y * lax.rsqrt(jnp.mean(y * y, axis=-1, keepdims=True) + EPS)
        o_ref[pl.ds(r, c), :] = (yn * g_ref[pl.ds(r, c), :].astype(F32)).astype(BF16)
        return carry

    lax.fori_loop(0, nc, finish, 0)


def _retention(raw, st_f, st_b, proj, batch, seq):
    n = batch * seq
    st_spec = pl.BlockSpec((1, 1, RET_DK, RET_DK), lambda b, h: (b, h, 0, 0))

    def col(tile):
        return pl.BlockSpec((seq, RET_DK), lambda b, h: (b, tile * RET_HEADS + h))

    return pl.pallas_call(
        _ret_kernel,
        grid=(batch, RET_HEADS),
        in_specs=[pl.BlockSpec(memory_space=pltpu.SMEM), st_spec, st_spec,
                  col(0), col(1), col(2), col(3)],
        out_specs=pl.BlockSpec((seq, RET_DK), lambda b, h: (b, h)),
        out_shape=jax.ShapeDtypeStruct((n, RET_HEADS * RET_DK), BF16),
        scratch_shapes=[pltpu.VMEM((RET_DK, RET_DK), F32), pltpu.VMEM((RET_DK, RET_DK), F32),
                        pltpu.VMEM((seq, RET_DK), F32)],
        compiler_params=_cparams("arbitrary", "arbitrary"),
        name="retention",
    )(raw, st_f, st_b, proj, proj, proj, proj)


CONV_TL = 256
CONV_HALO = 16
CONV_RB = 32


def _conv_kernel(prev_ref, cur_ref, next_ref, cw_ref, cb_ref, lg_ref, lb_ref, o_ref, pad_ref, *, seq):
    i = pl.program_id(0)
    tl = CONV_TL
    has_prev = (i * tl) % seq != 0
    has_next = ((i + 1) * tl) % seq != 0
    pad_ref[0:CONV_HALO, :] = jnp.where(has_prev, prev_ref[...].astype(F32), 0.0)
    pad_ref[CONV_HALO:CONV_HALO + tl, :] = cur_ref[...].astype(F32)
    pad_ref[CONV_HALO + tl:, :] = jnp.where(has_next, next_ref[...].astype(F32), 0.0)
    off = CONV_HALO - CONV_HALF
    for rb in range(tl // CONV_RB):
        base = rb * CONV_RB + off
        acc = pad_ref[base:base + CONV_RB, :] * cw_ref[0:1, :]
        for w in range(1, CONV_WIDTH):
            acc = acc + pad_ref[base + w:base + w + CONV_RB, :] * cw_ref[w:w + 1, :]
        u = acc + cb_ref[...]
        mu = jnp.mean(u, axis=-1, keepdims=True)
        uc = u - mu
        y = uc * lax.rsqrt(jnp.mean(uc * uc, axis=-1, keepdims=True) + EPS) * lg_ref[...] + lb_ref[...]
        o_ref[rb * CONV_RB:(rb + 1) * CONV_RB, :] = _silu(y).astype(BF16)


def _conv_branch(proj, conv_w, conv_b, ln_g, ln_b, seq):
    n = proj.shape[0]
    d = D_MODEL
    glu_tile = 4
    hb = CONV_TL // CONV_HALO
    n_halo = n // CONV_HALO
    vec = pl.BlockSpec((1, d), lambda i: (0, 0))
    return pl.pallas_call(
        functools.partial(_conv_kernel, seq=seq),
        grid=(n // CONV_TL,),
        in_specs=[pl.BlockSpec((CONV_HALO, d), lambda i: (jnp.maximum(i * hb - 1, 0), glu_tile)),
                  pl.BlockSpec((CONV_TL, d), lambda i: (i, glu_tile)),
                  pl.BlockSpec((CONV_HALO, d), lambda i: (jnp.minimum((i + 1) * hb, n_halo - 1), glu_tile)),
                  pl.BlockSpec((CONV_WIDTH, d), lambda i: (0, 0)),
                  vec, vec, vec],
        out_specs=pl.BlockSpec((CONV_TL, d), lambda i: (i, 0)),
        out_shape=jax.ShapeDtypeStruct((n, d), BF16),
        scratch_shapes=[pltpu.VMEM((CONV_TL + 2 * CONV_HALO, d), F32)],
        compiler_params=_cparams("arbitrary"),
        name="conv_branch",
    )(proj, proj, proj, conv_w, conv_b, ln_g, ln_b)


MERGE_TM = 256


def _merge_kernel(x_ref, yr_ref, uc_ref, ga_ref, gb_ref, wr_ref, wc_ref, wo_ref, wq_ref,
                  gain1_ref, gain2_ref, g1_ref, sh2_ref, sc2_ref, lat_ref, h2_ref, qp_ref):
    a = jnp.dot(yr_ref[...], wr_ref[...], preferred_element_type=F32)
    b = jnp.dot(uc_ref[...], wc_ref[...], preferred_element_type=F32)
    merged = ga_ref[...].astype(F32) * a + gb_ref[...].astype(F32) * b
    y = jnp.dot(merged.astype(BF16), wo_ref[...], preferred_element_type=F32)
    lat = x_ref[...] + g1_ref[0] * _rms(y, gain1_ref[...])
    lat_ref[...] = lat
    h2 = _rms(lat, gain2_ref[...]) * (1.0 + sc2_ref[0]) + sh2_ref[0]
    h2_ref[...] = _pack_rows(h2)
    qp_ref[...] = jnp.dot(h2.astype(BF16), wq_ref[...], preferred_element_type=F32).astype(BF16)


def _merge(x2, y_ret, u_conv, proj, w_ret_o, w_conv_o, w_out, w_q, gain1, gain2, g1, sh2, sc2, seq):
    n, d = x2.shape
    tps = seq // MERGE_TM
    nq = w_q.shape[1]
    row = pl.BlockSpec((MERGE_TM, d), lambda i: (i, 0))
    wsp = pl.BlockSpec((d, d), lambda i: (0, 0))
    vec = pl.BlockSpec((1, d), lambda i: (0, 0))
    bvec = pl.BlockSpec((1, 1, d), lambda i: (i // tps, 0, 0))
    return pl.pallas_call(
        _merge_kernel,
        grid=(n // MERGE_TM,),
        in_specs=[row, row, row,
                  pl.BlockSpec((MERGE_TM, d), lambda i: (i, 5)),
                  pl.BlockSpec((MERGE_TM, d), lambda i: (i, 6)),
                  wsp, wsp, wsp,
                  pl.BlockSpec((d, nq), lambda i: (0, 0)),
                  vec, vec, bvec, bvec, bvec],
        out_specs=[row, pl.BlockSpec((MERGE_TM, d // 2), lambda i: (i, 0)),
                   pl.BlockSpec((MERGE_TM, nq), lambda i: (i, 0))],
        out_shape=[jax.ShapeDtypeStruct((n, d), F32), jax.ShapeDtypeStruct((n, d // 2), jnp.int32),
                   jax.ShapeDtypeStruct((n, nq), BF16)],
        compiler_params=_cparams("arbitrary"),
        name="merge",
    )(x2, y_ret, u_conv, proj, proj, w_ret_o, w_conv_o, w_out, w_q, gain1, gain2, g1, sh2, sc2)


ROUTE_T = 512


def _topk_rows(s, k, payload=None):
    r = s.shape[0]
    rows = lax.broadcasted_iota(jnp.int32, s.shape, 0).astype(F32)
    vals, idxs = [], []
    for _ in range(k):
        m = jnp.max(s, axis=0, keepdims=True)
        pos = jnp.min(jnp.where(s == m, rows, float(r)), axis=0, keepdims=True)
        hit = rows == pos
        vals.append(m)
        if payload is None:
            idxs.append(pos)
        else:
            idxs.append(jnp.max(jnp.where(hit, payload, -1.0), axis=0, keepdims=True))
        s = jnp.where(hit, -jnp.inf, s)
    return jnp.concatenate(vals, axis=0), jnp.concatenate(idxs, axis=0)


def _route_kernel(q_ref, keys_ref, e_ref, g_ref, es_ref, gs_ref):
    h = pl.program_id(1)
    nt = (((1,), (1,)), ((), ()))
    row0 = pl.multiple_of(h * PEER_TOPK, PEER_TOPK)
    for t in range(ROUTE_T // LANE):
        cols = slice(t * LANE, (t + 1) * LANE)
        tops = []
        for p in range(2):
            qh = q_ref[cols, p * PEER_DK_HALF:(p + 1) * PEER_DK_HALF]
            s = lax.dot_general(keys_ref[0, p], qh, nt, preferred_element_type=F32)
            tops.append(_topk_rows(s, PEER_TOPK))
        (v1, i1), (v2, i2) = tops
        widths = [PEER_TOPK // (a + 1) for a in range(PEER_TOPK)]
        pad = (-sum(widths)) % 8
        cand = jnp.concatenate([v1[a:a + 1] + v2[:w] for a, w in enumerate(widths)]
                               + [jnp.full((pad, LANE), -jnp.inf, F32)], axis=0)
        cidx = jnp.concatenate([i1[a:a + 1] * float(PEER_N_KEYS) + i2[:w] for a, w in enumerate(widths)]
                               + [jnp.zeros((pad, LANE), F32)], axis=0)
        best, experts = _topk_rows(cand, PEER_TOPK, payload=cidx)
        ex = jnp.exp(best - best[0:1])
        gates = ex / jnp.sum(ex, axis=0, keepdims=True)
        es_ref[pl.ds(row0, PEER_TOPK), cols] = experts.astype(jnp.int32)
        gs_ref[pl.ds(row0, PEER_TOPK), cols] = gates

    @pl.when(h == PEER_HEADS - 1)
    def _():
        first_piece = es_ref[...].T * PIECES
        for s in range(PIECES):
            e_ref[:, s * PEER_SEL:(s + 1) * PEER_SEL] = first_piece + s
        g_ref[...] = gs_ref[...].T


def _route(qp, keys):
    n = qp.shape[0]
    hw = 2 * PEER_DK_HALF
    out = pl.BlockSpec((ROUTE_T, PEER_SEL), lambda i, h: (i, 0))
    return pl.pallas_call(
        _route_kernel,
        grid=(n // ROUTE_T, PEER_HEADS),
        in_specs=[pl.BlockSpec((ROUTE_T, hw), lambda i, h: (i, h)),
                  pl.BlockSpec((1, 2, PEER_N_KEYS, PEER_DK_HALF), lambda i, h: (h, 0, 0, 0))],
        out_specs=[pl.BlockSpec((ROUTE_T, PIECES * PEER_SEL), lambda i, h: (i, 0)), out],
        out_shape=[jax.ShapeDtypeStruct((n, PIECES * PEER_SEL), jnp.int32),
                   jax.ShapeDtypeStruct((n, PEER_SEL), F32)],
        scratch_shapes=[pltpu.VMEM((PEER_SEL, ROUTE_T), jnp.int32), pltpu.VMEM((PEER_SEL, ROUTE_T), F32)],
        compiler_params=_cparams("arbitrary", "arbitrary"),
        name="peer_route",
    )(qp, keys)


def _sc_mesh():
    return plsc.VectorSubcoreMesh(core_axis_name="c", subcore_axis_name="s")


def _worker_id():
    return lax.axis_index("s") * SC_CORES + lax.axis_index("c")


SC_NBUF = PEER_SEL // SC_ROWS
SC_QUAD = 4
SC_GROUP = 2
SC_WORDS = LANE // SC_LANES


def _sc_scratch():
    assert SC_NBUF == PIECES
    return [pltpu.VMEM((SC_NBUF, PIECES, SC_ROWS, LANE), jnp.int32)] + [pltpu.SemaphoreType.DMA] * SC_NBUF


def _gather(tab_hbm, idx_v, rows, sem, tl, ch, s):
    return pltpu.make_async_copy(tab_hbm.at[idx_v.at[tl, pl.ds(s * PEER_SEL + ch * SC_ROWS, SC_ROWS)]],
                                 rows.at[ch, s], sem)


def _widen_pair(packed_bf16):
    words = plsc.bitcast(packed_bf16, jnp.int32)
    return (plsc.bitcast(lax.shift_left(words, 16), F32), plsc.bitcast(words & HIGH_HALF, F32))


def _gelu_tanh_via_exp(x):
    z = 0.7978845608028654 * (x + 0.044715 * (x * x * x))
    return 0.5 * x * (2.0 - 2.0 / (jnp.exp(2.0 * z) + 1.0))


def _peer_experts_sc(u_pieces, v_pieces, xw, gates, piece_idx):
    n = xw.shape[0]
    per = n // SC_WORKERS
    d = PIECES * PACK_BLOCK

    @functools.partial(
        pl.kernel, mesh=_sc_mesh(),
        out_type=jax.ShapeDtypeStruct((n, d), F32),
        scratch_types=[pltpu.VMEM((SC_TB, PIECES * PEER_SEL), jnp.int32), pltpu.VMEM((SC_TB, PIECES * LANE), jnp.int32),
                       pltpu.VMEM((SC_TB, PEER_SEL), F32), pltpu.VMEM((SC_ROWS, SC_LANES), F32),
                       pltpu.VMEM((SC_TB, PEER_SEL), F32), pltpu.VMEM((SC_TB, PEER_SEL), jnp.int32),
                       pltpu.VMEM((SC_TB, d), F32)] + _sc_scratch() + [pltpu.SemaphoreType.DMA] * PIECES,
        compiler_params=pltpu.CompilerParams(needs_layout_passes=False),
        name="peer_experts_sc",
    )
    def run(u_hbm, v_hbm, x_hbm, g_hbm, idx_hbm, out_hbm, idx_v, x_v, res_v, part_v, g_v, w_v, acc_v, rows_all, *all_sems):
        sems, sems_v = all_sems[:SC_NBUF], all_sems[SC_NBUF:]
        base = _worker_id() * per
        lane = lax.iota(jnp.int32, SC_LANES)
        zero = jnp.zeros((SC_LANES,), F32)
        quads_per_chunk = SC_ROWS // SC_QUAD

        def gathers(tl, ch):
            return [_gather(u_hbm, idx_v, rows_all, sems[ch], tl, ch, s) for s in range(PIECES)]

        def gathers_v(tl, s):
            return [_gather(v_hbm, idx_v, rows_all, sems_v[s], tl, ch, s) for ch in range(SC_NBUF)]

        @pl.loop(0, per // SC_TB)
        def _(blk):
            tok0 = pl.multiple_of(base + blk * SC_TB, SC_TB)
            pltpu.sync_copy(idx_hbm.at[pl.ds(tok0, SC_TB)], idx_v)
            pltpu.sync_copy(x_hbm.at[pl.ds(tok0, SC_TB)], x_v)
            for ch in range(SC_NBUF):
                for cp in gathers(0, ch):
                    cp.start()
            pltpu.sync_copy(g_hbm.at[pl.ds(tok0, SC_TB)], g_v)

            @pl.loop(0, SC_TB)
            def _(tl):
                xs = [plsc.bitcast(x_v[tl, pl.ds(i * SC_LANES, SC_LANES)], BF16) for i in range(PIECES * SC_WORDS)]
                for ch in range(SC_NBUF):
                    for cp in gathers(tl, ch):
                        cp.wait()
                    rows = rows_all.at[ch]

                    @pl.loop(0, SC_ROWS, step=SC_GROUP)
                    def _(r0):
                        for r in range(SC_GROUP):
                            acc = zero
                            for s in range(PIECES):
                                for w0 in range(0, SC_WORDS, SC_QUAD):
                                    part = None
                                    for j in range(w0, w0 + SC_QUAD):
                                        prod = plsc.bitcast(rows[s, r0 + r, pl.ds(j * SC_LANES, SC_LANES)],
                                                            BF16) * xs[s * SC_WORDS + j]
                                        part = prod if part is None else part + prod
                                    lo, hi = _widen_pair(part)
                                    acc = acc + lo + hi
                            part_v[r0 + r, :] = acc
                    for g in range(SC_ROWS // SC_LANES):
                        row_ids = lane + g * SC_LANES
                        res = zero
                        for col in range(SC_LANES):
                            res = res + plsc.load_gather(part_v, [row_ids, jnp.full((SC_LANES,), col, jnp.int32)])
                        res_v[tl, pl.ds(ch * SC_ROWS + g * SC_LANES, SC_LANES)] = res

                    @pl.when(tl + 1 < SC_TB)
                    def _():
                        for cp in gathers(tl + 1, ch):
                            cp.start()

            for s in range(PIECES):
                for cp in gathers_v(0, s):
                    cp.start()

            @pl.loop(0, SC_TB)
            def _(t):
                for c in range(0, PEER_SEL, SC_LANES):
                    w = g_v[t, pl.ds(c, SC_LANES)] * _gelu_tanh_via_exp(res_v[t, pl.ds(c, SC_LANES)])
                    w_v[t, pl.ds(c, SC_LANES)] = plsc.bitcast(plsc.pack(w, w, format=plsc.PackFormat.INTERLEAVED),
                                                             jnp.int32)

            @pl.loop(0, SC_TB)
            def _(tl):
                tl_vec = jnp.full((SC_LANES,), 0, jnp.int32) + tl
                for s in range(PIECES):
                    for cp in gathers_v(tl, s):
                        cp.wait()

                    def row_quad(q, accs):
                        ch = lax.shift_right_logical(q, quads_per_chunk.bit_length() - 1)
                        r0 = pl.multiple_of((q & (quads_per_chunk - 1)) * SC_QUAD, SC_QUAD)
                        quad = rows_all.at[ch, s, pl.ds(r0, SC_QUAD)]
                        col0 = jnp.full((SC_LANES,), 0, jnp.int32) + q * SC_QUAD
                        ws = [plsc.bitcast(plsc.load_gather(w_v, [tl_vec, col0 + j]), BF16) for j in range(SC_QUAD)]
                        accs = list(accs)
                        for k in range(SC_WORDS):
                            part = None
                            for j in range(SC_QUAD):
                                prod = plsc.bitcast(quad[j, pl.ds(k * SC_LANES, SC_LANES)], BF16) * ws[j]
                                part = prod if part is None else part + prod
                            lo, hi = _widen_pair(part)
                            accs[k] = accs[k] + lo
                            accs[SC_WORDS + k] = accs[SC_WORDS + k] + hi
                        return tuple(accs)

                    accs = lax.fori_loop(0, PEER_SEL // SC_QUAD, row_quad, (zero,) * (2 * SC_WORDS))
                    for i, acc in enumerate(accs):
                        acc_v[tl, pl.ds(s * PACK_BLOCK + i * SC_LANES, SC_LANES)] = acc

                    @pl.when(tl + 1 < SC_TB)
                    def _():
                        for cp in gathers_v(tl + 1, s):
                            cp.start()

            pltpu.sync_copy(acc_v, out_hbm.at[pl.ds(tok0, SC_TB)])

    return run(u_pieces, v_pieces, xw, gates, piece_idx)


FIN_TM = 1024


def _final_kernel(lat_ref, p_ref, gain_ref, g2_ref, o_ref):
    o_ref[...] = lat_ref[...] + g2_ref[0] * _rms(p_ref[...], gain_ref[...])


def _final(lat, peer, gain3, g2, seq):
    n, d = lat.shape
    tps = seq // FIN_TM
    row = pl.BlockSpec((FIN_TM, d), lambda i: (i, 0))
    return pl.pallas_call(
        _final_kernel, grid=(n // FIN_TM,),
        in_specs=[row, row, pl.BlockSpec((1, d), lambda i: (0, 0)),
                  pl.BlockSpec((1, 1, d), lambda i: (i // tps, 0, 0))],
        out_specs=row, out_shape=jax.ShapeDtypeStruct((n, d), F32),
        compiler_params=_cparams("arbitrary"), name="final_residual",
    )(lat, peer, gain3, g2)


def _expert_pieces(table):
    e, d = table.shape
    bits = lax.bitcast_convert_type(table.astype(BF16), jnp.uint16).astype(jnp.uint32)
    bits = bits.reshape(e, d // PACK_BLOCK, 2, LANE)
    words = bits[:, :, 0, :] | (bits[:, :, 1, :] << 16)
    return lax.bitcast_convert_type(words, jnp.int32).reshape(e * (d // PACK_BLOCK), LANE)


def _axial_rotary(seq):
    n_rows = seq // GRID_W
    rows = jnp.repeat(jnp.arange(n_rows, dtype=F32), GRID_W)
    cols = jnp.tile(jnp.arange(GRID_W, dtype=F32), n_rows)
    quarter = RET_DK // 4
    inv = ROPE_BASE ** (-jnp.arange(quarter, dtype=F32) / quarter)
    ang = jnp.concatenate([rows[:, None] * inv, cols[:, None] * inv], axis=-1)
    return jnp.cos(ang), jnp.sin(ang)


def kernel(x, c, ctx, c_ctx, w_mod, b_mod, norm_gain, w_in, ret_decay_raw, w_ret_o, conv_w, conv_b,
           conv_norm_g, conv_norm_b, w_conv_o, w_out, peer_wq, peer_keys, peer_u, peer_v):
    assert w_mod.shape[0] == 1, "single layer"
    batch, seq, d = x.shape
    n = batch * seq
    assert d == D_MODEL and seq % IN_TM == 0 and seq % (SC_WORKERS * SC_TB) == 0
    gain = norm_gain[0]
    x2 = x.reshape(n, d)

    pad = (-(batch + 1)) % 8
    c_all = jnp.concatenate([c, c_ctx[None, :], jnp.zeros((pad, d), F32)], axis=0)
    mod = _modulation(c_all, w_mod[0], b_mod[0][None, :])
    mod_lat = mod[:batch].reshape(batch, N_MOD, 1, d)
    sh1, sc1, g1, sh2, sc2, g2 = (mod_lat[:, i] for i in range(N_MOD))
    mod_ctx = mod[batch:batch + 1].reshape(N_MOD, 1, d)

    w_in0 = w_in[0]
    q_w = RET_HEADS * RET_DK
    w_kv = w_in0[:, q_w:3 * q_w].astype(BF16)
    st_f, st_b = _context_states(ret_decay_raw[0], ctx, gain[0:1], mod_ctx[0], mod_ctx[1], w_kv)

    hw = d // 2
    glu_a = w_in0[:, 4 * d:5 * d]
    glu_b = w_in0[:, 5 * d:6 * d]
    w_perm = jnp.concatenate([w_in0[:, :4 * d], glu_a[:, :hw], glu_b[:, :hw], glu_a[:, hw:], glu_b[:, hw:],
                              w_in0[:, 6 * d:]], axis=1).astype(BF16)
    cos, sin = _axial_rotary(seq)
    w_ret_b, w_conv_b, w_out_b, w_q_b = (w[0].astype(BF16) for w in (w_ret_o, w_conv_o, w_out, peer_wq))
    keys_b = peer_keys[0].astype(BF16)
    u_pieces = _expert_pieces(peer_u[0])
    v_pieces = _expert_pieces(peer_v[0])
    conv_vecs = (conv_b[0][None, :], conv_norm_g[0][None, :], conv_norm_b[0][None, :])

    groups = BATCH_GROUPS if sum(BATCH_GROUPS) == batch else (batch,)
    for tile_rows in (IN_TM, CONV_TL, MERGE_TM, ROUTE_T, FIN_TM, SC_WORKERS * SC_TB):
        assert seq % tile_rows == 0, "token count per sequence must be a multiple of every row tile"
    outs = []
    b0 = 0
    after = jnp.zeros((8, PEER_SEL), F32)
    for gb in groups:
        bs = slice(b0, b0 + gb)
        xg = x2[b0 * seq:(b0 + gb) * seq]
        b0 += gb
        proj = _in_projection(xg, gain[0:1], sh1[bs], sc1[bs], cos, sin, w_perm, after, seq)
        y_ret = _retention(ret_decay_raw[0], st_f[bs], st_b[bs], proj, gb, seq)
        u_conv = _conv_branch(proj, conv_w[0], *conv_vecs, seq)
        lat1, h2_words, qp = _merge(xg, y_ret, u_conv, proj, w_ret_b, w_conv_b, w_out_b, w_q_b, gain[1:2],
                                    gain[2:3], g1[bs], sh2[bs], sc2[bs], seq)
        piece_idx, gates = _route(qp, keys_b)
        after = gates[:8]
        peer_out = _peer_experts_sc(u_pieces, v_pieces, h2_words, gates, piece_idx)
        outs.append(_final(lat1, peer_out, gain[3:4], g2[bs], seq))
    return jnp.concatenate(outs, axis=0).reshape(batch, seq, d)
```

```python
import functools

import jax
import jax.numpy as jnp
from jax import lax
from jax.experimental import pallas as pl
from jax.experimental.pallas import tpu as pltpu
from jax.experimental.pallas import tpu_sc as plsc

F32 = jnp.float32
BF16 = jnp.bfloat16

D_MODEL = 1024
GRID_W = 64
EPS = 1e-6
N_MOD = 6
RET_HEADS = 4
RET_DK = 256
RET_CHUNK = 128
ROPE_BASE = 10000.0
CONV_WIDTH = 31
CONV_HALF = CONV_WIDTH // 2
PEER_HEADS = 8
PEER_N_KEYS = 128
PEER_DK_HALF = 128
PEER_TOPK = 16
PEER_SEL = PEER_HEADS * PEER_TOPK

SC_CORES = 2
SC_SUBCORES = 16
SC_WORKERS = SC_CORES * SC_SUBCORES
SC_LANES = 16
SC_ROWS = 32
SC_TB = 16
LANE = 128
PACK_BLOCK = 2 * LANE
PIECES = D_MODEL // PACK_BLOCK
HIGH_HALF = -65536

BATCH_GROUPS = (1, 1, 2, 3, 4, 5)

VMEM_LIMIT = 48 * 1024 * 1024


def _cparams(*sem):
    return pltpu.CompilerParams(dimension_semantics=sem, vmem_limit_bytes=VMEM_LIMIT)


def _sigmoid(x):
    return 1.0 / (1.0 + jnp.exp(-x))


def _silu(x):
    return x * _sigmoid(x)


def _softplus(x):
    return jnp.maximum(x, 0.0) + jnp.log1p(jnp.exp(-jnp.abs(x)))


def _rms(x, gain):
    return x * lax.rsqrt(jnp.mean(x * x, axis=-1, keepdims=True) + EPS) * gain


def _bf16_bits_high(x):
    return lax.bitcast_convert_type(x.astype(BF16).astype(F32), jnp.int32)


def _pack_bf16_pair(lo, hi):
    return _bf16_bits_high(hi) | lax.shift_right_logical(_bf16_bits_high(lo), 16)


def _pack_rows(x):
    parts = [_pack_bf16_pair(x[:, b:b + LANE], x[:, b + LANE:b + PACK_BLOCK])
             for b in range(0, x.shape[1], PACK_BLOCK)]
    return jnp.concatenate(parts, axis=1)


def _mod_kernel(c_ref, w_ref, b_ref, o_ref):
    a = _silu(c_ref[...])
    o_ref[...] = jnp.dot(a, w_ref[...], preferred_element_type=F32,
                         precision=lax.Precision.HIGHEST) + b_ref[...]


def _modulation(c_all, w_mod, b_mod):
    rows, d = c_all.shape
    n = w_mod.shape[1]
    return pl.pallas_call(
        _mod_kernel,
        grid=(n // d,),
        in_specs=[pl.BlockSpec((rows, d), lambda j: (0, 0)),
                  pl.BlockSpec((d, d), lambda j: (0, j)),
                  pl.BlockSpec((1, d), lambda j: (0, j))],
        out_specs=pl.BlockSpec((rows, d), lambda j: (0, j)),
        out_shape=jax.ShapeDtypeStruct((rows, n), F32),
        compiler_params=_cparams("arbitrary"),
        name="modulation",
    )(c_all, w_mod, b_mod)


def _ctx_kernel(raw_ref, ctx_ref, gain_ref, sh_ref, sc_ref, wkv_ref, sf_ref, sb_ref):
    x = ctx_ref[0]
    lc = x.shape[0]
    hc = _rms(x, gain_ref[...]) * (1.0 + sc_ref[...]) + sh_ref[...]
    kv = jnp.dot(hc.astype(BF16), wkv_ref[...], preferred_element_type=F32)
    pos = lax.broadcasted_iota(jnp.int32, (lc, RET_DK), 0).astype(F32)
    tn = (((0,), (0,)), ((), ()))
    for h in range(RET_HEADS):
        lgf = -_softplus(jnp.full((lc, RET_DK), raw_ref[0, h], F32))
        lgb = -_softplus(jnp.full((lc, RET_DK), raw_ref[1, h], F32))
        k = kv[:, h * RET_DK:(h + 1) * RET_DK] * (RET_DK ** -0.5)
        v = kv[:, (RET_HEADS + h) * RET_DK:(RET_HEADS + h + 1) * RET_DK].astype(BF16)
        kf = (k * jnp.exp(lgf * (lc - 1.0 - pos))).astype(BF16)
        kb = (k * jnp.exp(lgb * pos)).astype(BF16)
        sf_ref[0, h] = lax.dot_general(kf, v, tn, preferred_element_type=F32)
        sb_ref[0, h] = lax.dot_general(kb, v, tn, preferred_element_type=F32)


def _context_states(raw, ctx, gain0, csh, csc, w_kv):
    b, lc, d = ctx.shape
    st = jax.ShapeDtypeStruct((b, RET_HEADS, RET_DK, RET_DK), F32)
    st_spec = pl.BlockSpec((1, RET_HEADS, RET_DK, RET_DK), lambda i: (i, 0, 0, 0))
    vec = pl.BlockSpec((1, d), lambda i: (0, 0))
    return pl.pallas_call(
        _ctx_kernel,
        grid=(b,),
        in_specs=[pl.BlockSpec(memory_space=pltpu.SMEM),
                  pl.BlockSpec((1, lc, d), lambda i: (i, 0, 0)),
                  vec, vec, vec,
                  pl.BlockSpec(w_kv.shape, lambda i: (0, 0))],
        out_specs=[st_spec, st_spec],
        out_shape=[st, st],
        compiler_params=_cparams("arbitrary"),
        name="context_states",
    )(raw, ctx, gain0, csh, csc, w_kv)


IN_TM = 1024
IN_TILES = 8
IN_OUT_TILES = 7


def _inproj_kernel(x_ref, gain_ref, sh_ref, sc_ref, cos_ref, sin_ref, w_ref, after_ref, o_ref, xn_ref):
    del after_ref
    j = pl.program_id(1)

    @pl.when(j == 0)
    def _():
        h = _rms(x_ref[...], gain_ref[...]) * (1.0 + sc_ref[0]) + sh_ref[0]
        xn_ref[...] = h.astype(BF16)

    acc = jnp.dot(xn_ref[...], w_ref[...], preferred_element_type=F32)
    half = RET_DK // 2

    def rotary(scale):
        cos = cos_ref[...]
        sin = sin_ref[...]
        for h in range(RET_HEADS):
            t1 = acc[:, h * RET_DK:h * RET_DK + half]
            t2 = acc[:, h * RET_DK + half:(h + 1) * RET_DK]
            o_ref[:, h * RET_DK:h * RET_DK + half] = ((t1 * cos - t2 * sin) * scale).astype(BF16)
            o_ref[:, h * RET_DK + half:(h + 1) * RET_DK] = ((t1 * sin + t2 * cos) * scale).astype(BF16)

    @pl.when(j == 0)
    def _():
        rotary(1.0)

    @pl.when(j == 1)
    def _():
        rotary(RET_DK ** -0.5)

    @pl.when(j == 2)
    def _():
        o_ref[...] = acc.astype(BF16)

    @pl.when(j == 3)
    def _():
        o_ref[...] = _silu(acc).astype(BF16)

    hw = D_MODEL // 2

    @pl.when(j == 4)
    def _():
        o_ref[:, :hw] = (acc[:, :hw] * _sigmoid(acc[:, hw:])).astype(BF16)

    @pl.when(j == 5)
    def _():
        o_ref[:, hw:] = (acc[:, :hw] * _sigmoid(acc[:, hw:])).astype(BF16)

    @pl.when(j >= 6)
    def _():
        o_ref[...] = _sigmoid(acc).astype(BF16)


def _out_tile(j):
    return jnp.where(j <= 4, j, j - 1)


def _in_projection(x2, gain0, sh1, sc1, cos, sin, w_perm, after, seq):
    n, d = x2.shape
    tiles_per_seq = seq // IN_TM
    return pl.pallas_call(
        _inproj_kernel,
        grid=(n // IN_TM, IN_TILES),
        in_specs=[pl.BlockSpec((IN_TM, d), lambda i, j: (i, 0)),
                  pl.BlockSpec((1, d), lambda i, j: (0, 0)),
                  pl.BlockSpec((1, 1, d), lambda i, j: (i // tiles_per_seq, 0, 0)),
                  pl.BlockSpec((1, 1, d), lambda i, j: (i // tiles_per_seq, 0, 0)),
                  pl.BlockSpec((IN_TM, RET_DK // 2), lambda i, j: (i % tiles_per_seq, 0)),
                  pl.BlockSpec((IN_TM, RET_DK // 2), lambda i, j: (i % tiles_per_seq, 0)),
                  pl.BlockSpec((d, d), lambda i, j: (0, j)),
                  pl.BlockSpec(after.shape, lambda i, j: (0, 0))],
        out_specs=pl.BlockSpec((IN_TM, d), lambda i, j: (i, _out_tile(j))),
        out_shape=jax.ShapeDtypeStruct((n, IN_OUT_TILES * d), BF16),
        scratch_shapes=[pltpu.VMEM((IN_TM, d), BF16)],
        compiler_params=_cparams("arbitrary", "arbitrary"),
        name="in_projection",
    )(x2, gain0, sh1, sc1, cos, sin, w_perm, after)


def _ret_kernel(raw_ref, s0f_ref, s0b_ref, q_ref, k_ref, v_ref, g_ref, o_ref, sf_ref, sb_ref, y_ref):
    hd = pl.program_id(1)
    c = RET_CHUNK
    seq = q_ref.shape[0]
    nc = seq // c

    def log_gamma(direction, shape):
        return -_softplus(jnp.full(shape, raw_ref[direction, hd], F32))

    ri = lax.broadcasted_iota(jnp.int32, (c, c), 0).astype(F32)
    ci = lax.broadcasted_iota(jnp.int32, (c, c), 1).astype(F32)
    dec_f = jnp.where(ri >= ci, jnp.exp(log_gamma(0, (c, c)) * jnp.maximum(ri - ci, 0.0)), 0.0)
    dec_b = jnp.where(ci >= ri, jnp.exp(log_gamma(1, (c, c)) * jnp.maximum(ci - ri, 0.0)), 0.0)
    pos = lax.broadcasted_iota(jnp.int32, (c, RET_DK), 0).astype(F32)
    lgf = log_gamma(0, (c, RET_DK))
    lgb = log_gamma(1, (c, RET_DK))
    qdec_f = jnp.exp(lgf * (pos + 1.0))
    kdec_f = jnp.exp(lgf * (c - 1.0 - pos))
    qdec_b = jnp.exp(lgb * (c - pos))
    kdec_b = jnp.exp(lgb * pos)
    cdec_f = jnp.exp(log_gamma(0, (1, RET_DK)) * c)
    cdec_b = jnp.exp(log_gamma(1, (1, RET_DK)) * c)
    sf_ref[...] = s0f_ref[0, 0]
    sb_ref[...] = s0b_ref[0, 0]
    nt = (((1,), (1,)), ((), ()))
    tn = (((0,), (0,)), ((), ()))

    def chunk(st_ref, row, dec, qdec, kdec, cdec):
        q = q_ref[pl.ds(row, c), :]
        k = k_ref[pl.ds(row, c), :]
        v = v_ref[pl.ds(row, c), :]
        s = lax.dot_general(q, k, nt, preferred_element_type=F32) * dec
        st = st_ref[...]
        y = jnp.dot(s.astype(BF16), v, preferred_element_type=F32)
        y = y + jnp.dot((q.astype(F32) * qdec).astype(BF16), st.astype(BF16), preferred_element_type=F32)
        kd = (k.astype(F32) * kdec).astype(BF16)
        st_ref[...] = st * cdec + lax.dot_general(kd, v, tn, preferred_element_type=F32)
        return y

    def first_half(i, carry):
        rf = pl.multiple_of(i * c, c)
        rb = pl.multiple_of((nc - 1 - i) * c, c)
        y_ref[pl.ds(rf, c), :] = chunk(sf_ref, rf, dec_f, qdec_f, kdec_f, cdec_f)
        y_ref[pl.ds(rb, c), :] = chunk(sb_ref, rb, dec_b, qdec_b, kdec_b, cdec_b)
        return carry

    def second_half(i, carry):
        rf = pl.multiple_of(i * c, c)
        rb = pl.multiple_of((nc - 1 - i) * c, c)
        y_ref[pl.ds(rf, c), :] += chunk(sf_ref, rf, dec_f, qdec_f, kdec_f, cdec_f)
        y_ref[pl.ds(rb, c), :] += chunk(sb_ref, rb, dec_b, qdec_b, kdec_b, cdec_b)
        return carry

    lax.fori_loop(0, nc // 2, first_half, 0)
    lax.fori_loop(nc // 2, nc, second_half, 0)

    def finish(i, carry):
        r = pl.multiple_of(i * c, c)
        y = y_ref[pl.ds(r, c), :]
        yn = y * lax.rsqrt(jnp.mean(y * y, axis=-1, keepdims=True) + EPS)
        o_ref[pl.ds(r, c), :] = (yn * g_ref[pl.ds(r, c), :].astype(F32)).astype(BF16)
        return carry

    lax.fori_loop(0, nc, finish, 0)


def _retention(raw, st_f, st_b, proj, batch, seq):
    n = batch * seq
    st_spec = pl.BlockSpec((1, 1, RET_DK, RET_DK), lambda b, h: (b, h, 0, 0))

    def col(tile):
        return pl.BlockSpec((seq, RET_DK), lambda b, h: (b, tile * RET_HEADS + h))

    return pl.pallas_call(
        _ret_kernel,
        grid=(batch, RET_HEADS),
        in_specs=[pl.BlockSpec(memory_space=pltpu.SMEM), st_spec, st_spec,
                  col(0), col(1), col(2), col(3)],
        out_specs=pl.BlockSpec((seq, RET_DK), lambda b, h: (b, h)),
        out_shape=jax.ShapeDtypeStruct((n, RET_HEADS * RET_DK), BF16),
        scratch_shapes=[pltpu.VMEM((RET_DK, RET_DK), F32), pltpu.VMEM((RET_DK, RET_DK), F32),
                        pltpu.VMEM((seq, RET_DK), F32)],
        compiler_params=_cparams("arbitrary", "arbitrary"),
        name="retention",
    )(raw, st_f, st_b, proj, proj, proj, proj)


CONV_TL = 256
CONV_HALO = 16
CONV_RB = 32


def _conv_kernel(prev_ref, cur_ref, next_ref, cw_ref, cb_ref, lg_ref, lb_ref, o_ref, pad_ref, *, seq):
    i = pl.program_id(0)
    tl = CONV_TL
    has_prev = (i * tl) % seq != 0
    has_next = ((i + 1) * tl) % seq != 0
    pad_ref[0:CONV_HALO, :] = jnp.where(has_prev, prev_ref[...].astype(F32), 0.0)
    pad_ref[CONV_HALO:CONV_HALO + tl, :] = cur_ref[...].astype(F32)
    pad_ref[CONV_HALO + tl:, :] = jnp.where(has_next, next_ref[...].astype(F32), 0.0)
    off = CONV_HALO - CONV_HALF
    for rb in range(tl // CONV_RB):
        base = rb * CONV_RB + off
        acc = pad_ref[base:base + CONV_RB, :] * cw_ref[0:1, :]
        for w in range(1, CONV_WIDTH):
            acc = acc + pad_ref[base + w:base + w + CONV_RB, :] * cw_ref[w:w + 1, :]
        u = acc + cb_ref[...]
        mu = jnp.mean(u, axis=-1, keepdims=True)
        uc = u - mu
        y = uc * lax.rsqrt(jnp.mean(uc * uc, axis=-1, keepdims=True) + EPS) * lg_ref[...] + lb_ref[...]
        o_ref[rb * CONV_RB:(rb + 1) * CONV_RB, :] = _silu(y).astype(BF16)


def _conv_branch(proj, conv_w, conv_b, ln_g, ln_b, seq):
    n = proj.shape[0]
    d = D_MODEL
    glu_tile = 4
    hb = CONV_TL // CONV_HALO
    n_halo = n // CONV_HALO
    vec = pl.BlockSpec((1, d), lambda i: (0, 0))
    return pl.pallas_call(
        functools.partial(_conv_kernel, seq=seq),
        grid=(n // CONV_TL,),
        in_specs=[pl.BlockSpec((CONV_HALO, d), lambda i: (jnp.maximum(i * hb - 1, 0), glu_tile)),
                  pl.BlockSpec((CONV_TL, d), lambda i: (i, glu_tile)),
                  pl.BlockSpec((CONV_HALO, d), lambda i: (jnp.minimum((i + 1) * hb, n_halo - 1), glu_tile)),
                  pl.BlockSpec((CONV_WIDTH, d), lambda i: (0, 0)),
                  vec, vec, vec],
        out_specs=pl.BlockSpec((CONV_TL, d), lambda i: (i, 0)),
        out_shape=jax.ShapeDtypeStruct((n, d), BF16),
        scratch_shapes=[pltpu.VMEM((CONV_TL + 2 * CONV_HALO, d), F32)],
        compiler_params=_cparams("arbitrary"),
        name="conv_branch",
    )(proj, proj, proj, conv_w, conv_b, ln_g, ln_b)


MERGE_TM = 256


def _merge_kernel(x_ref, yr_ref, uc_ref, ga_ref, gb_ref, wr_ref, wc_ref, wo_ref, wq_ref,
                  gain1_ref, gain2_ref, g1_ref, sh2_ref, sc2_ref, lat_ref, h2_ref, qp_ref):
    a = jnp.dot(yr_ref[...], wr_ref[...], preferred_element_type=F32)
    b = jnp.dot(uc_ref[...], wc_ref[...], preferred_element_type=F32)
    merged = ga_ref[...].astype(F32) * a + gb_ref[...].astype(F32) * b
    y = jnp.dot(merged.astype(BF16), wo_ref[...], preferred_element_type=F32)
    lat = x_ref[...] + g1_ref[0] * _rms(y, gain1_ref[...])
    lat_ref[...] = lat
    h2 = _rms(lat, gain2_ref[...]) * (1.0 + sc2_ref[0]) + sh2_ref[0]
    h2_ref[...] = _pack_rows(h2)
    qp_ref[...] = jnp.dot(h2.astype(BF16), wq_ref[...], preferred_element_type=F32).astype(BF16)


def _merge(x2, y_ret, u_conv, proj, w_ret_o, w_conv_o, w_out, w_q, gain1, gain2, g1, sh2, sc2, seq):
    n, d = x2.shape
    tps = seq // MERGE_TM
    nq = w_q.shape[1]
    row = pl.BlockSpec((MERGE_TM, d), lambda i: (i, 0))
    wsp = pl.BlockSpec((d, d), lambda i: (0, 0))
    vec = pl.BlockSpec((1, d), lambda i: (0, 0))
    bvec = pl.BlockSpec((1, 1, d), lambda i: (i // tps, 0, 0))
    return pl.pallas_call(
        _merge_kernel,
        grid=(n // MERGE_TM,),
        in_specs=[row, row, row,
                  pl.BlockSpec((MERGE_TM, d), lambda i: (i, 5)),
                  pl.BlockSpec((MERGE_TM, d), lambda i: (i, 6)),
                  wsp, wsp, wsp,
                  pl.BlockSpec((d, nq), lambda i: (0, 0)),
                  vec, vec, bvec, bvec, bvec],
        out_specs=[row, pl.BlockSpec((MERGE_TM, d // 2), lambda i: (i, 0)),
                   pl.BlockSpec((MERGE_TM, nq), lambda i: (i, 0))],
        out_shape=[jax.ShapeDtypeStruct((n, d), F32), jax.ShapeDtypeStruct((n, d // 2), jnp.int32),
                   jax.ShapeDtypeStruct((n, nq), BF16)],
        compiler_params=_cparams("arbitrary"),
        name="merge",
    )(x2, y_ret, u_conv, proj, proj, w_ret_o, w_conv_o, w_out, w_q, gain1, gain2, g1, sh2, sc2)


ROUTE_T = 512


def _topk_rows(s, k, payload=None):
    r = s.shape[0]
    rows = lax.broadcasted_iota(jnp.int32, s.shape, 0).astype(F32)
    vals, idxs = [], []
    for _ in range(k):
        m = jnp.max(s, axis=0, keepdims=True)
        pos = jnp.min(jnp.where(s == m, rows, float(r)), axis=0, keepdims=True)
        hit = rows == pos
        vals.append(m)
        if payload is None:
            idxs.append(pos)
        else:
            idxs.append(jnp.max(jnp.where(hit, payload, -1.0), axis=0, keepdims=True))
        s = jnp.where(hit, -jnp.inf, s)
    return jnp.concatenate(vals, axis=0), jnp.concatenate(idxs, axis=0)


def _route_kernel(q_ref, keys_ref, e_ref, g_ref, es_ref, gs_ref):
    h = pl.program_id(1)
    nt = (((1,), (1,)), ((), ()))
    row0 = pl.multiple_of(h * PEER_TOPK, PEER_TOPK)
    for t in range(ROUTE_T // LANE):
        cols = slice(t * LANE, (t + 1) * LANE)
        tops = []
        for p in range(2):
            qh = q_ref[cols, p * PEER_DK_HALF:(p + 1) * PEER_DK_HALF]
            s = lax.dot_general(keys_ref[0, p], qh, nt, preferred_element_type=F32)
            tops.append(_topk_rows(s, PEER_TOPK))
        (v1, i1), (v2, i2) = tops
        widths = [PEER_TOPK // (a + 1) for a in range(PEER_TOPK)]
        pad = (-sum(widths)) % 8
        cand = jnp.concatenate([v1[a:a + 1] + v2[:w] for a, w in enumerate(widths)]
                               + [jnp.full((pad, LANE), -jnp.inf, F32)], axis=0)
        cidx = jnp.concatenate([i1[a:a + 1] * float(PEER_N_KEYS) + i2[:w] for a, w in enumerate(widths)]
                               + [jnp.zeros((pad, LANE), F32)], axis=0)
        best, experts = _topk_rows(cand, PEER_TOPK, payload=cidx)
        ex = jnp.exp(best - best[0:1])
        gates = ex / jnp.sum(ex, axis=0, keepdims=True)
        es_ref[pl.ds(row0, PEER_TOPK), cols] = experts.astype(jnp.int32)
        gs_ref[pl.ds(row0, PEER_TOPK), cols] = gates

    @pl.when(h == PEER_HEADS - 1)
    def _():
        first_piece = es_ref[...].T * PIECES
        for s in range(PIECES):
            e_ref[:, s * PEER_SEL:(s + 1) * PEER_SEL] = first_piece + s
        g_ref[...] = gs_ref[...].T


def _route(qp, keys):
    n = qp.shape[0]
    hw = 2 * PEER_DK_HALF
    out = pl.BlockSpec((ROUTE_T, PEER_SEL), lambda i, h: (i, 0))
    return pl.pallas_call(
        _route_kernel,
        grid=(n // ROUTE_T, PEER_HEADS),
        in_specs=[pl.BlockSpec((ROUTE_T, hw), lambda i, h: (i, h)),
                  pl.BlockSpec((1, 2, PEER_N_KEYS, PEER_DK_HALF), lambda i, h: (h, 0, 0, 0))],
        out_specs=[pl.BlockSpec((ROUTE_T, PIECES * PEER_SEL), lambda i, h: (i, 0)), out],
        out_shape=[jax.ShapeDtypeStruct((n, PIECES * PEER_SEL), jnp.int32),
                   jax.ShapeDtypeStruct((n, PEER_SEL), F32)],
        scratch_shapes=[pltpu.VMEM((PEER_SEL, ROUTE_T), jnp.int32), pltpu.VMEM((PEER_SEL, ROUTE_T), F32)],
        compiler_params=_cparams("arbitrary", "arbitrary"),
        name="peer_route",
    )(qp, keys)


def _sc_mesh():
    return plsc.VectorSubcoreMesh(core_axis_name="c", subcore_axis_name="s")


def _worker_id():
    return lax.axis_index("s") * SC_CORES + lax.axis_index("c")


SC_NBUF = PEER_SEL // SC_ROWS
SC_QUAD = 4
SC_GROUP = 2
SC_WORDS = LANE // SC_LANES


def _sc_scratch():
    assert SC_NBUF == PIECES
    return [pltpu.VMEM((SC_NBUF, PIECES, SC_ROWS, LANE), jnp.int32)] + [pltpu.SemaphoreType.DMA] * SC_NBUF


def _gather(tab_hbm, idx_v, rows, sem, tl, ch, s):
    return pltpu.make_async_copy(tab_hbm.at[idx_v.at[tl, pl.ds(s * PEER_SEL + ch * SC_ROWS, SC_ROWS)]],
                                 rows.at[ch, s], sem)


def _widen_pair(packed_bf16):
    words = plsc.bitcast(packed_bf16, jnp.int32)
    return (plsc.bitcast(lax.shift_left(words, 16), F32), plsc.bitcast(words & HIGH_HALF, F32))


def _gelu_tanh_via_exp(x):
    z = 0.7978845608028654 * (x + 0.044715 * (x * x * x))
    return 0.5 * x * (2.0 - 2.0 / (jnp.exp(2.0 * z) + 1.0))


def _peer_experts_sc(u_pieces, v_pieces, xw, gates, piece_idx):
    n = xw.shape[0]
    per = n // SC_WORKERS
    d = PIECES * PACK_BLOCK

    @functools.partial(
        pl.kernel, mesh=_sc_mesh(),
        out_type=jax.ShapeDtypeStruct((n, d), F32),
        scratch_types=[pltpu.VMEM((SC_TB, PIECES * PEER_SEL), jnp.int32), pltpu.VMEM((SC_TB, PIECES * LANE), jnp.int32),
                       pltpu.VMEM((SC_TB, PEER_SEL), F32), pltpu.VMEM((SC_ROWS, SC_LANES), F32),
                       pltpu.VMEM((SC_TB, PEER_SEL), F32), pltpu.VMEM((SC_TB, PEER_SEL), jnp.int32),
                       pltpu.VMEM((SC_TB, d), F32)] + _sc_scratch() + [pltpu.SemaphoreType.DMA] * PIECES,
        compiler_params=pltpu.CompilerParams(needs_layout_passes=False),
        name="peer_experts_sc",
    )
    def run(u_hbm, v_hbm, x_hbm, g_hbm, idx_hbm, out_hbm, idx_v, x_v, res_v, part_v, g_v, w_v, acc_v,
            rows_all, *all_sems):
        sems, sems_v = all_sems[:SC_NBUF], all_sems[SC_NBUF:]
        base = _worker_id() * per
        lane = lax.iota(jnp.int32, SC_LANES)
        zero = jnp.zeros((SC_LANES,), F32)
        quads_per_chunk = SC_ROWS // SC_QUAD

        def gathers(tl, ch):
            return [_gather(u_hbm, idx_v, rows_all, sems[ch], tl, ch, s) for s in range(PIECES)]

        def gathers_v(tl, s):
            return [_gather(v_hbm, idx_v, rows_all, sems_v[s], tl, ch, s) for ch in range(SC_NBUF)]

        @pl.loop(0, per // SC_TB)
        def _(blk):
            tok0 = pl.multiple_of(base + blk * SC_TB, SC_TB)
            pltpu.sync_copy(idx_hbm.at[pl.ds(tok0, SC_TB)], idx_v)
            for ch in range(SC_NBUF):
                for cp in gathers(0, ch):
                    cp.start()
            pltpu.sync_copy(x_hbm.at[pl.ds(tok0, SC_TB)], x_v)
            pltpu.sync_copy(g_hbm.at[pl.ds(tok0, SC_TB)], g_v)

            @pl.loop(0, SC_TB)
            def _(tl):
                xs = [plsc.bitcast(x_v[tl, pl.ds(i * SC_LANES, SC_LANES)], BF16) for i in range(PIECES * SC_WORDS)]
                for ch in range(SC_NBUF):
                    for cp in gathers(tl, ch):
                        cp.wait()
                    rows = rows_all.at[ch]

                    @pl.loop(0, SC_ROWS, step=SC_GROUP)
                    def _(r0):
                        for r in range(SC_GROUP):
                            acc = zero
                            for s in range(PIECES):
                                for w0 in range(0, SC_WORDS, SC_QUAD):
                                    part = None
                                    for j in range(w0, w0 + SC_QUAD):
                                        prod = plsc.bitcast(rows[s, r0 + r, pl.ds(j * SC_LANES, SC_LANES)],
                                                            BF16) * xs[s * SC_WORDS + j]
                                        part = prod if part is None else part + prod
                                    lo, hi = _widen_pair(part)
                                    acc = acc + lo + hi
                            part_v[r0 + r, :] = acc
                    for g in range(SC_ROWS // SC_LANES):
                        row_ids = lane + g * SC_LANES
                        res = zero
                        for col in range(SC_LANES):
                            res = res + plsc.load_gather(part_v, [row_ids, jnp.full((SC_LANES,), col, jnp.int32)])
                        res_v[tl, pl.ds(ch * SC_ROWS + g * SC_LANES, SC_LANES)] = res

                    @pl.when(tl + 1 < SC_TB)
                    def _():
                        for cp in gathers(tl + 1, ch):
                            cp.start()

                    @pl.when(tl + 1 == SC_TB)
                    def _():
                        for s in range(PIECES):
                            _gather(v_hbm, idx_v, rows_all, sems_v[s], 0, ch, s).start()

            @pl.loop(0, SC_TB)
            def _(t):
                for c in range(0, PEER_SEL, SC_LANES):
                    w = g_v[t, pl.ds(c, SC_LANES)] * _gelu_tanh_via_exp(res_v[t, pl.ds(c, SC_LANES)])
                    w_v[t, pl.ds(c, SC_LANES)] = plsc.bitcast(plsc.pack(w, w, format=plsc.PackFormat.INTERLEAVED),
                                                             jnp.int32)

            @pl.loop(0, SC_TB)
            def _(tl):
                tl_vec = jnp.full((SC_LANES,), 0, jnp.int32) + tl
                for s in range(PIECES):
                    for cp in gathers_v(tl, s):
                        cp.wait()

                    def row_quad(q, accs):
                        ch = lax.shift_right_logical(q, quads_per_chunk.bit_length() - 1)
                        r0 = pl.multiple_of((q & (quads_per_chunk - 1)) * SC_QUAD, SC_QUAD)
                        quad = rows_all.at[ch, s, pl.ds(r0, SC_QUAD)]
                        col0 = jnp.full((SC_LANES,), 0, jnp.int32) + q * SC_QUAD
                        ws = [plsc.bitcast(plsc.load_gather(w_v, [tl_vec, col0 + j]), BF16) for j in range(SC_QUAD)]
                        accs = list(accs)
                        for k in range(SC_WORDS):
                            part = None
                            for j in range(SC_QUAD):
                                prod = plsc.bitcast(quad[j, pl.ds(k * SC_LANES, SC_LANES)], BF16) * ws[j]
                                part = prod if part is None else part + prod
                            lo, hi = _widen_pair(part)
                            accs[k] = accs[k] + lo
                            accs[SC_WORDS + k] = accs[SC_WORDS + k] + hi
                        return tuple(accs)

                    accs = lax.fori_loop(0, PEER_SEL // SC_QUAD, row_quad, (zero,) * (2 * SC_WORDS))
                    for i, acc in enumerate(accs):
                        acc_v[tl, pl.ds(s * PACK_BLOCK + i * SC_LANES, SC_LANES)] = acc

                    @pl.when(tl + 1 < SC_TB)
                    def _():
                        for cp in gathers_v(tl + 1, s):
                            cp.start()

            pltpu.sync_copy(acc_v, out_hbm.at[pl.ds(tok0, SC_TB)])

    return run(u_pieces, v_pieces, xw, gates, piece_idx)


FIN_TM = 1024


def _final_kernel(lat_ref, p_ref, gain_ref, g2_ref, *rest):
    o_ref = rest[-1]
    o_ref[...] = lat_ref[...] + g2_ref[0] * _rms(p_ref[...], gain_ref[...])


def _final(lat, peer, gain3, g2, out_all, row0, n_total, seq):
    n, d = lat.shape
    tps = seq // FIN_TM
    row = pl.BlockSpec((FIN_TM, d), lambda i: (i, 0))
    ins = [lat, peer, gain3, g2] + ([] if out_all is None else [out_all])
    return pl.pallas_call(
        _final_kernel, grid=(n // FIN_TM,),
        in_specs=[row, row, pl.BlockSpec((1, d), lambda i: (0, 0)),
                  pl.BlockSpec((1, 1, d), lambda i: (i // tps, 0, 0))]
        + ([] if out_all is None else [pl.BlockSpec(memory_space=pl.ANY)]),
        out_specs=pl.BlockSpec((FIN_TM, d), lambda i: (i + row0 // FIN_TM, 0)),
        out_shape=jax.ShapeDtypeStruct((n_total, d), F32),
        input_output_aliases={} if out_all is None else {len(ins) - 1: 0},
        compiler_params=_cparams("arbitrary"), name="final_residual",
    )(*ins)


def _expert_pieces(table):
    e, d = table.shape
    bits = lax.bitcast_convert_type(table.astype(BF16), jnp.uint16).astype(jnp.uint32)
    bits = bits.reshape(e, d // PACK_BLOCK, 2, LANE)
    words = bits[:, :, 0, :] | (bits[:, :, 1, :] << 16)
    return lax.bitcast_convert_type(words, jnp.int32).reshape(e * (d // PACK_BLOCK), LANE)


def _axial_rotary(seq):
    n_rows = seq // GRID_W
    rows = jnp.repeat(jnp.arange(n_rows, dtype=F32), GRID_W)
    cols = jnp.tile(jnp.arange(GRID_W, dtype=F32), n_rows)
    quarter = RET_DK // 4
    inv = ROPE_BASE ** (-jnp.arange(quarter, dtype=F32) / quarter)
    ang = jnp.concatenate([rows[:, None] * inv, cols[:, None] * inv], axis=-1)
    return jnp.cos(ang), jnp.sin(ang)


def kernel(x, c, ctx, c_ctx, w_mod, b_mod, norm_gain, w_in, ret_decay_raw, w_ret_o, conv_w, conv_b,
           conv_norm_g, conv_norm_b, w_conv_o, w_out, peer_wq, peer_keys, peer_u, peer_v):
    assert w_mod.shape[0] == 1, "single layer"
    batch, seq, d = x.shape
    n = batch * seq
    assert d == D_MODEL and seq % IN_TM == 0 and seq % (SC_WORKERS * SC_TB) == 0
    gain = norm_gain[0]
    x2 = x.reshape(n, d)

    pad = (-(batch + 1)) % 8
    c_all = jnp.concatenate([c, c_ctx[None, :], jnp.zeros((pad, d), F32)], axis=0)
    mod = _modulation(c_all, w_mod[0], b_mod[0][None, :])
    mod_lat = mod[:batch].reshape(batch, N_MOD, 1, d)
    sh1, sc1, g1, sh2, sc2, g2 = (mod_lat[:, i] for i in range(N_MOD))
    mod_ctx = mod[batch:batch + 1].reshape(N_MOD, 1, d)

    w_in0 = w_in[0]
    q_w = RET_HEADS * RET_DK
    w_kv = w_in0[:, q_w:3 * q_w].astype(BF16)
    st_f, st_b = _context_states(ret_decay_raw[0], ctx, gain[0:1], mod_ctx[0], mod_ctx[1], w_kv)

    hw = d // 2
    glu_a = w_in0[:, 4 * d:5 * d]
    glu_b = w_in0[:, 5 * d:6 * d]
    w_perm = jnp.concatenate([w_in0[:, :4 * d], glu_a[:, :hw], glu_b[:, :hw], glu_a[:, hw:], glu_b[:, hw:],
                              w_in0[:, 6 * d:]], axis=1).astype(BF16)
    cos, sin = _axial_rotary(seq)
    w_ret_b, w_conv_b, w_out_b, w_q_b = (w[0].astype(BF16) for w in (w_ret_o, w_conv_o, w_out, peer_wq))
    keys_b = peer_keys[0].astype(BF16)
    u_pieces = _expert_pieces(peer_u[0])
    v_pieces = _expert_pieces(peer_v[0])
    conv_vecs = (conv_b[0][None, :], conv_norm_g[0][None, :], conv_norm_b[0][None, :])

    groups = BATCH_GROUPS if sum(BATCH_GROUPS) == batch else (batch,)
    for tile_rows in (IN_TM, CONV_TL, MERGE_TM, ROUTE_T, FIN_TM, SC_WORKERS * SC_TB):
        assert seq % tile_rows == 0, "token count per sequence must be a multiple of every row tile"
    out = None
    b0 = 0
    after = jnp.zeros((8, PEER_SEL), F32)
    for gb in groups:
        bs = slice(b0, b0 + gb)
        row0 = b0 * seq
        xg = x2[row0:row0 + gb * seq]
        b0 += gb
        proj = _in_projection(xg, gain[0:1], sh1[bs], sc1[bs], cos, sin, w_perm, after, seq)
        y_ret = _retention(ret_decay_raw[0], st_f[bs], st_b[bs], proj, gb, seq)
        u_conv = _conv_branch(proj, conv_w[0], *conv_vecs, seq)
        lat1, h2_words, qp = _merge(xg, y_ret, u_conv, proj, w_ret_b, w_conv_b, w_out_b, w_q_b, gain[1:2],
                                    gain[2:3], g1[bs], sh2[bs], sc2[bs], seq)
        piece_idx, gates = _route(qp, keys_b)
        after = gates[:8]
        peer_out = _peer_experts_sc(u_pieces, v_pieces, h2_words, gates, piece_idx)
        out = _final(lat1, peer_out, gain[3:4], g2[bs], out, row0, n, seq)
    return out.reshape(batch, seq, d)
```

```python
import functools

import jax
import jax.numpy as jnp
from jax import lax
from jax.experimental import pallas as pl
from jax.experimental.pallas import tpu as pltpu
from jax.experimental.pallas import tpu_sc as plsc

F32 = jnp.float32
BF16 = jnp.bfloat16

D_MODEL = 1024
GRID_W = 64
EPS = 1e-6
N_MOD = 6
RET_HEADS = 4
RET_DK = 256
RET_CHUNK = 128
ROPE_BASE = 10000.0
CONV_WIDTH = 31
CONV_HALF = CONV_WIDTH // 2
PEER_HEADS = 8
PEER_N_KEYS = 128
PEER_DK_HALF = 128
PEER_TOPK = 16
PEER_SEL = PEER_HEADS * PEER_TOPK

SC_CORES = 2
SC_SUBCORES = 16
SC_WORKERS = SC_CORES * SC_SUBCORES
SC_LANES = 16
SC_ROWS = 32
SC_TB = 16
LANE = 128
PACK_BLOCK = 2 * LANE
PIECES = D_MODEL // PACK_BLOCK
HIGH_HALF = -65536

BATCH_GROUPS = (1, 1, 2, 3, 4, 5)

VMEM_LIMIT = 48 * 1024 * 1024


def _cparams(*sem):
    return pltpu.CompilerParams(dimension_semantics=sem, vmem_limit_bytes=VMEM_LIMIT)


def _sigmoid(x):
    return 1.0 / (1.0 + jnp.exp(-x))


def _silu(x):
    return x * _sigmoid(x)


def _softplus(x):
    return jnp.maximum(x, 0.0) + jnp.log1p(jnp.exp(-jnp.abs(x)))


def _rms(x, gain):
    return x * lax.rsqrt(jnp.mean(x * x, axis=-1, keepdims=True) + EPS) * gain


def _bf16_bits_high(x):
    return lax.bitcast_convert_type(x.astype(BF16).astype(F32), jnp.int32)


def _pack_bf16_pair(lo, hi):
    return _bf16_bits_high(hi) | lax.shift_right_logical(_bf16_bits_high(lo), 16)


def _pack_rows(x):
    parts = [_pack_bf16_pair(x[:, b:b + LANE], x[:, b + LANE:b + PACK_BLOCK])
             for b in range(0, x.shape[1], PACK_BLOCK)]
    return jnp.concatenate(parts, axis=1)


def _mod_kernel(c_ref, w_ref, b_ref, o_ref):
    a = _silu(c_ref[...])
    o_ref[...] = jnp.dot(a, w_ref[...], preferred_element_type=F32,
                         precision=lax.Precision.HIGHEST) + b_ref[...]


def _modulation(c_all, w_mod, b_mod):
    rows, d = c_all.shape
    n = w_mod.shape[1]
    return pl.pallas_call(
        _mod_kernel,
        grid=(n // d,),
        in_specs=[pl.BlockSpec((rows, d), lambda j: (0, 0)),
                  pl.BlockSpec((d, d), lambda j: (0, j)),
                  pl.BlockSpec((1, d), lambda j: (0, j))],
        out_specs=pl.BlockSpec((rows, d), lambda j: (0, j)),
        out_shape=jax.ShapeDtypeStruct((rows, n), F32),
        compiler_params=_cparams("arbitrary"),
        name="modulation",
    )(c_all, w_mod, b_mod)


def _ctx_kernel(raw_ref, ctx_ref, gain_ref, sh_ref, sc_ref, wkv_ref, sf_ref, sb_ref):
    x = ctx_ref[0]
    lc = x.shape[0]
    hc = _rms(x, gain_ref[...]) * (1.0 + sc_ref[...]) + sh_ref[...]
    kv = jnp.dot(hc.astype(BF16), wkv_ref[...], preferred_element_type=F32)
    pos = lax.broadcasted_iota(jnp.int32, (lc, RET_DK), 0).astype(F32)
    tn = (((0,), (0,)), ((), ()))
    for h in range(RET_HEADS):
        lgf = -_softplus(jnp.full((lc, RET_DK), raw_ref[0, h], F32))
        lgb = -_softplus(jnp.full((lc, RET_DK), raw_ref[1, h], F32))
        k = kv[:, h * RET_DK:(h + 1) * RET_DK] * (RET_DK ** -0.5)
        v = kv[:, (RET_HEADS + h) * RET_DK:(RET_HEADS + h + 1) * RET_DK].astype(BF16)
        kf = (k * jnp.exp(lgf * (lc - 1.0 - pos))).astype(BF16)
        kb = (k * jnp.exp(lgb * pos)).astype(BF16)
        sf_ref[0, h] = lax.dot_general(kf, v, tn, preferred_element_type=F32)
        sb_ref[0, h] = lax.dot_general(kb, v, tn, preferred_element_type=F32)


def _context_states(raw, ctx, gain0, csh, csc, w_kv):
    b, lc, d = ctx.shape
    st = jax.ShapeDtypeStruct((b, RET_HEADS, RET_DK, RET_DK), F32)
    st_spec = pl.BlockSpec((1, RET_HEADS, RET_DK, RET_DK), lambda i: (i, 0, 0, 0))
    vec = pl.BlockSpec((1, d), lambda i: (0, 0))
    return pl.pallas_call(
        _ctx_kernel,
        grid=(b,),
        in_specs=[pl.BlockSpec(memory_space=pltpu.SMEM),
                  pl.BlockSpec((1, lc, d), lambda i: (i, 0, 0)),
                  vec, vec, vec,
                  pl.BlockSpec(w_kv.shape, lambda i: (0, 0))],
        out_specs=[st_spec, st_spec],
        out_shape=[st, st],
        compiler_params=_cparams("arbitrary"),
        name="context_states",
    )(raw, ctx, gain0, csh, csc, w_kv)


IN_TM = 1024
IN_TILES = 8
IN_OUT_TILES = 7


def _inproj_kernel(x_ref, gain_ref, sh_ref, sc_ref, cos_ref, sin_ref, w_ref, after_ref, o_ref, xn_ref):
    del after_ref
    j = pl.program_id(1)

    @pl.when(j == 0)
    def _():
        h = _rms(x_ref[...], gain_ref[...]) * (1.0 + sc_ref[0]) + sh_ref[0]
        xn_ref[...] = h.astype(BF16)

    acc = jnp.dot(xn_ref[...], w_ref[...], preferred_element_type=F32)
    half = RET_DK // 2

    def rotary(scale):
        cos = cos_ref[...]
        sin = sin_ref[...]
        for h in range(RET_HEADS):
            t1 = acc[:, h * RET_DK:h * RET_DK + half]
            t2 = acc[:, h * RET_DK + half:(h + 1) * RET_DK]
            o_ref[:, h * RET_DK:h * RET_DK + half] = ((t1 * cos - t2 * sin) * scale).astype(BF16)
            o_ref[:, h * RET_DK + half:(h + 1) * RET_DK] = ((t1 * sin + t2 * cos) * scale).astype(BF16)

    @pl.when(j == 0)
    def _():
        rotary(1.0)

    @pl.when(j == 1)
    def _():
        rotary(RET_DK ** -0.5)

    @pl.when(j == 2)
    def _():
        o_ref[...] = acc.astype(BF16)

    @pl.when(j == 3)
    def _():
        o_ref[...] = _silu(acc).astype(BF16)

    hw = D_MODEL // 2

    @pl.when(j == 4)
    def _():
        o_ref[:, :hw] = (acc[:, :hw] * _sigmoid(acc[:, hw:])).astype(BF16)

    @pl.when(j == 5)
    def _():
        o_ref[:, hw:] = (acc[:, :hw] * _sigmoid(acc[:, hw:])).astype(BF16)

    @pl.when(j >= 6)
    def _():
        o_ref[...] = _sigmoid(acc).astype(BF16)


def _out_tile(j):
    return jnp.where(j <= 4, j, j - 1)


def _in_projection(x2, gain0, sh1, sc1, cos, sin, w_perm, after, seq):
    n, d = x2.shape
    tiles_per_seq = seq // IN_TM
    return pl.pallas_call(
        _inproj_kernel,
        grid=(n // IN_TM, IN_TILES),
        in_specs=[pl.BlockSpec((IN_TM, d), lambda i, j: (i, 0)),
                  pl.BlockSpec((1, d), lambda i, j: (0, 0)),
                  pl.BlockSpec((1, 1, d), lambda i, j: (i // tiles_per_seq, 0, 0)),
                  pl.BlockSpec((1, 1, d), lambda i, j: (i // tiles_per_seq, 0, 0)),
                  pl.BlockSpec((IN_TM, RET_DK // 2), lambda i, j: (i % tiles_per_seq, 0)),
                  pl.BlockSpec((IN_TM, RET_DK // 2), lambda i, j: (i % tiles_per_seq, 0)),
                  pl.BlockSpec((d, d), lambda i, j: (0, j)),
                  pl.BlockSpec(after.shape, lambda i, j: (0, 0))],
        out_specs=pl.BlockSpec((IN_TM, d), lambda i, j: (i, _out_tile(j))),
        out_shape=jax.ShapeDtypeStruct((n, IN_OUT_TILES * d), BF16),
        scratch_shapes=[pltpu.VMEM((IN_TM, d), BF16)],
        compiler_params=_cparams("arbitrary", "arbitrary"),
        name="in_projection",
    )(x2, gain0, sh1, sc1, cos, sin, w_perm, after)


def _ret_kernel(raw_ref, s0f_ref, s0b_ref, q_ref, k_ref, v_ref, g_ref, o_ref, sf_ref, sb_ref, y_ref):
    hd = pl.program_id(1)
    c = RET_CHUNK
    seq = q_ref.shape[0]
    nc = seq // c

    def log_gamma(direction, shape):
        return -_softplus(jnp.full(shape, raw_ref[direction, hd], F32))

    ri = lax.broadcasted_iota(jnp.int32, (c, c), 0).astype(F32)
    ci = lax.broadcasted_iota(jnp.int32, (c, c), 1).astype(F32)
    dec_f = jnp.where(ri >= ci, jnp.exp(log_gamma(0, (c, c)) * jnp.maximum(ri - ci, 0.0)), 0.0)
    dec_b = jnp.where(ci >= ri, jnp.exp(log_gamma(1, (c, c)) * jnp.maximum(ci - ri, 0.0)), 0.0)
    pos = lax.broadcasted_iota(jnp.int32, (c, RET_DK), 0).astype(F32)
    lgf = log_gamma(0, (c, RET_DK))
    lgb = log_gamma(1, (c, RET_DK))
    qdec_f = jnp.exp(lgf * (pos + 1.0))
    kdec_f = jnp.exp(lgf * (c - 1.0 - pos))
    qdec_b = jnp.exp(lgb * (c - pos))
    kdec_b = jnp.exp(lgb * pos)
    cdec_f = jnp.exp(log_gamma(0, (1, RET_DK)) * c)
    cdec_b = jnp.exp(log_gamma(1, (1, RET_DK)) * c)
    sf_ref[...] = s0f_ref[0, 0]
    sb_ref[...] = s0b_ref[0, 0]
    nt = (((1,), (1,)), ((), ()))
    tn = (((0,), (0,)), ((), ()))

    def chunk(st_ref, row, dec, qdec, kdec, cdec):
        q = q_ref[pl.ds(row, c), :]
        k = k_ref[pl.ds(row, c), :]
        v = v_ref[pl.ds(row, c), :]
        s = lax.dot_general(q, k, nt, preferred_element_type=F32) * dec
        st = st_ref[...]
        y = jnp.dot(s.astype(BF16), v, preferred_element_type=F32)
        y = y + jnp.dot((q.astype(F32) * qdec).astype(BF16), st.astype(BF16), preferred_element_type=F32)
        kd = (k.astype(F32) * kdec).astype(BF16)
        st_ref[...] = st * cdec + lax.dot_general(kd, v, tn, preferred_element_type=F32)
        return y

    def first_half(i, carry):
        rf = pl.multiple_of(i * c, c)
        rb = pl.multiple_of((nc - 1 - i) * c, c)
        y_ref[pl.ds(rf, c), :] = chunk(sf_ref, rf, dec_f, qdec_f, kdec_f, cdec_f)
        y_ref[pl.ds(rb, c), :] = chunk(sb_ref, rb, dec_b, qdec_b, kdec_b, cdec_b)
        return carry

    def second_half(i, carry):
        rf = pl.multiple_of(i * c, c)
        rb = pl.multiple_of((nc - 1 - i) * c, c)
        y_ref[pl.ds(rf, c), :] += chunk(sf_ref, rf, dec_f, qdec_f, kdec_f, cdec_f)
        y_ref[pl.ds(rb, c), :] += chunk(sb_ref, rb, dec_b, qdec_b, kdec_b, cdec_b)
        return carry

    lax.fori_loop(0, nc // 2, first_half, 0)
    lax.fori_loop(nc // 2, nc, second_half, 0)

    def finish(i, carry):
        r = pl.multiple_of(i * c, c)
        y = y_ref[pl.ds(r, c), :]
        yn = y * lax.rsqrt(jnp.mean(y * y, axis=-1, keepdims=True) + EPS)
        o_ref[pl.ds(r, c), :] = (yn * g_ref[pl.ds(r, c), :].astype(F32)).astype(BF16)
        return carry

    lax.fori_loop(0, nc, finish, 0)


def _retention(raw, st_f, st_b, proj, batch, seq):
    n = batch * seq
    st_spec = pl.BlockSpec((1, 1, RET_DK, RET_DK), lambda b, h: (b, h, 0, 0))

    def col(tile):
        return pl.BlockSpec((seq, RET_DK), lambda b, h: (b, tile * RET_HEADS + h))

    return pl.pallas_call(
        _ret_kernel,
        grid=(batch, RET_HEADS),
        in_specs=[pl.BlockSpec(memory_space=pltpu.SMEM), st_spec, st_spec,
                  col(0), col(1), col(2), col(3)],
        out_specs=pl.BlockSpec((seq, RET_DK), lambda b, h: (b, h)),
        out_shape=jax.ShapeDtypeStruct((n, RET_HEADS * RET_DK), BF16),
        scratch_shapes=[pltpu.VMEM((RET_DK, RET_DK), F32), pltpu.VMEM((RET_DK, RET_DK), F32),
                        pltpu.VMEM((seq, RET_DK), F32)],
        compiler_params=_cparams("arbitrary", "arbitrary"),
        name="retention",
    )(raw, st_f, st_b, proj, proj, proj, proj)


CONV_TL = 256
CONV_HALO = 16
CONV_RB = 32


def _conv_kernel(prev_ref, cur_ref, next_ref, cw_ref, cb_ref, lg_ref, lb_ref, o_ref, pad_ref, *, seq):
    i = pl.program_id(0)
    tl = CONV_TL
    has_prev = (i * tl) % seq != 0
    has_next = ((i + 1) * tl) % seq != 0
    pad_ref[0:CONV_HALO, :] = jnp.where(has_prev, prev_ref[...].astype(F32), 0.0)
    pad_ref[CONV_HALO:CONV_HALO + tl, :] = cur_ref[...].astype(F32)
    pad_ref[CONV_HALO + tl:, :] = jnp.where(has_next, next_ref[...].astype(F32), 0.0)
    off = CONV_HALO - CONV_HALF
    for rb in range(tl // CONV_RB):
        base = rb * CONV_RB + off
        acc = pad_ref[base:base + CONV_RB, :] * cw_ref[0:1, :]
        for w in range(1, CONV_WIDTH):
            acc = acc + pad_ref[base + w:base + w + CONV_RB, :] * cw_ref[w:w + 1, :]
        u = acc + cb_ref[...]
        mu = jnp.mean(u, axis=-1, keepdims=True)
        uc = u - mu
        y = uc * lax.rsqrt(jnp.mean(uc * uc, axis=-1, keepdims=True) + EPS) * lg_ref[...] + lb_ref[...]
        o_ref[rb * CONV_RB:(rb + 1) * CONV_RB, :] = _silu(y).astype(BF16)


def _conv_branch(proj, conv_w, conv_b, ln_g, ln_b, seq):
    n = proj.shape[0]
    d = D_MODEL
    glu_tile = 4
    hb = CONV_TL // CONV_HALO
    n_halo = n // CONV_HALO
    vec = pl.BlockSpec((1, d), lambda i: (0, 0))
    return pl.pallas_call(
        functools.partial(_conv_kernel, seq=seq),
        grid=(n // CONV_TL,),
        in_specs=[pl.BlockSpec((CONV_HALO, d), lambda i: (jnp.maximum(i * hb - 1, 0), glu_tile)),
                  pl.BlockSpec((CONV_TL, d), lambda i: (i, glu_tile)),
                  pl.BlockSpec((CONV_HALO, d), lambda i: (jnp.minimum((i + 1) * hb, n_halo - 1), glu_tile)),
                  pl.BlockSpec((CONV_WIDTH, d), lambda i: (0, 0)),
                  vec, vec, vec],
        out_specs=pl.BlockSpec((CONV_TL, d), lambda i: (i, 0)),
        out_shape=jax.ShapeDtypeStruct((n, d), BF16),
        scratch_shapes=[pltpu.VMEM((CONV_TL + 2 * CONV_HALO, d), F32)],
        compiler_params=_cparams("arbitrary"),
        name="conv_branch",
    )(proj, proj, proj, conv_w, conv_b, ln_g, ln_b)


MERGE_TM = 256


def _merge_kernel(x_ref, yr_ref, uc_ref, ga_ref, gb_ref, wr_ref, wc_ref, wo_ref, wq_ref,
                  gain1_ref, gain2_ref, g1_ref, sh2_ref, sc2_ref, lat_ref, h2_ref, qp_ref):
    a = jnp.dot(yr_ref[...], wr_ref[...], preferred_element_type=F32)
    b = jnp.dot(uc_ref[...], wc_ref[...], preferred_element_type=F32)
    merged = ga_ref[...].astype(F32) * a + gb_ref[...].astype(F32) * b
    y = jnp.dot(merged.astype(BF16), wo_ref[...], preferred_element_type=F32)
    lat = x_ref[...] + g1_ref[0] * _rms(y, gain1_ref[...])
    lat_ref[...] = lat
    h2 = _rms(lat, gain2_ref[...]) * (1.0 + sc2_ref[0]) + sh2_ref[0]
    h2_ref[...] = _pack_rows(h2)
    qp_ref[...] = jnp.dot(h2.astype(BF16), wq_ref[...], preferred_element_type=F32).astype(BF16)


def _merge(x2, y_ret, u_conv, proj, w_ret_o, w_conv_o, w_out, w_q, gain1, gain2, g1, sh2, sc2, seq):
    n, d = x2.shape
    tps = seq // MERGE_TM
    nq = w_q.shape[1]
    row = pl.BlockSpec((MERGE_TM, d), lambda i: (i, 0))
    wsp = pl.BlockSpec((d, d), lambda i: (0, 0))
    vec = pl.BlockSpec((1, d), lambda i: (0, 0))
    bvec = pl.BlockSpec((1, 1, d), lambda i: (i // tps, 0, 0))
    return pl.pallas_call(
        _merge_kernel,
        grid=(n // MERGE_TM,),
        in_specs=[row, row, row,
                  pl.BlockSpec((MERGE_TM, d), lambda i: (i, 5)),
                  pl.BlockSpec((MERGE_TM, d), lambda i: (i, 6)),
                  wsp, wsp, wsp,
                  pl.BlockSpec((d, nq), lambda i: (0, 0)),
                  vec, vec, bvec, bvec, bvec],
        out_specs=[row, pl.BlockSpec((MERGE_TM, d // 2), lambda i: (i, 0)),
                   pl.BlockSpec((MERGE_TM, nq), lambda i: (i, 0))],
        out_shape=[jax.ShapeDtypeStruct((n, d), F32), jax.ShapeDtypeStruct((n, d // 2), jnp.int32),
                   jax.ShapeDtypeStruct((n, nq), BF16)],
        compiler_params=_cparams("arbitrary"),
        name="merge",
    )(x2, y_ret, u_conv, proj, proj, w_ret_o, w_conv_o, w_out, w_q, gain1, gain2, g1, sh2, sc2)


ROUTE_T = 512


def _topk_rows(s, k, payload=None):
    r = s.shape[0]
    rows = lax.broadcasted_iota(jnp.int32, s.shape, 0).astype(F32)
    vals, idxs = [], []
    for _ in range(k):
        m = jnp.max(s, axis=0, keepdims=True)
        pos = jnp.min(jnp.where(s == m, rows, float(r)), axis=0, keepdims=True)
        hit = rows == pos
        vals.append(m)
        if payload is None:
            idxs.append(pos)
        else:
            idxs.append(jnp.max(jnp.where(hit, payload, -1.0), axis=0, keepdims=True))
        s = jnp.where(hit, -jnp.inf, s)
    return jnp.concatenate(vals, axis=0), jnp.concatenate(idxs, axis=0)


def _route_kernel(q_ref, keys_ref, e_ref, g_ref, es_ref, gs_ref):
    h = pl.program_id(1)
    nt = (((1,), (1,)), ((), ()))
    row0 = pl.multiple_of(h * PEER_TOPK, PEER_TOPK)
    for t in range(ROUTE_T // LANE):
        cols = slice(t * LANE, (t + 1) * LANE)
        tops = []
        for p in range(2):
            qh = q_ref[cols, p * PEER_DK_HALF:(p + 1) * PEER_DK_HALF]
            s = lax.dot_general(keys_ref[0, p], qh, nt, preferred_element_type=F32)
            tops.append(_topk_rows(s, PEER_TOPK))
        (v1, i1), (v2, i2) = tops
        widths = [PEER_TOPK // (a + 1) for a in range(PEER_TOPK)]
        pad = (-sum(widths)) % 8
        cand = jnp.concatenate([v1[a:a + 1] + v2[:w] for a, w in enumerate(widths)]
                               + [jnp.full((pad, LANE), -jnp.inf, F32)], axis=0)
        cidx = jnp.concatenate([i1[a:a + 1] * float(PEER_N_KEYS) + i2[:w] for a, w in enumerate(widths)]
                               + [jnp.zeros((pad, LANE), F32)], axis=0)
        best, experts = _topk_rows(cand, PEER_TOPK, payload=cidx)
        ex = jnp.exp(best - best[0:1])
        gates = ex / jnp.sum(ex, axis=0, keepdims=True)
        es_ref[pl.ds(row0, PEER_TOPK), cols] = experts.astype(jnp.int32)
        gs_ref[pl.ds(row0, PEER_TOPK), cols] = gates

    @pl.when(h == PEER_HEADS - 1)
    def _():
        first_piece = es_ref[...].T * PIECES
        for s in range(PIECES):
            e_ref[:, s * PEER_SEL:(s + 1) * PEER_SEL] = first_piece + s
        g_ref[...] = gs_ref[...].T


def _route(qp, keys):
    n = qp.shape[0]
    hw = 2 * PEER_DK_HALF
    out = pl.BlockSpec((ROUTE_T, PEER_SEL), lambda i, h: (i, 0))
    return pl.pallas_call(
        _route_kernel,
        grid=(n // ROUTE_T, PEER_HEADS),
        in_specs=[pl.BlockSpec((ROUTE_T, hw), lambda i, h: (i, h)),
                  pl.BlockSpec((1, 2, PEER_N_KEYS, PEER_DK_HALF), lambda i, h: (h, 0, 0, 0))],
        out_specs=[pl.BlockSpec((ROUTE_T, PIECES * PEER_SEL), lambda i, h: (i, 0)), out],
        out_shape=[jax.ShapeDtypeStruct((n, PIECES * PEER_SEL), jnp.int32),
                   jax.ShapeDtypeStruct((n, PEER_SEL), F32)],
        scratch_shapes=[pltpu.VMEM((PEER_SEL, ROUTE_T), jnp.int32), pltpu.VMEM((PEER_SEL, ROUTE_T), F32)],
        compiler_params=_cparams("arbitrary", "arbitrary"),
        name="peer_route",
    )(qp, keys)


def _sc_mesh():
    return plsc.VectorSubcoreMesh(core_axis_name="c", subcore_axis_name="s")


def _worker_id():
    return lax.axis_index("s") * SC_CORES + lax.axis_index("c")


SC_NBUF = PEER_SEL // SC_ROWS
SC_QUAD = 4
SC_GROUP = 2
SC_WORDS = LANE // SC_LANES


def _sc_scratch():
    assert SC_NBUF == PIECES
    return [pltpu.VMEM((SC_NBUF, PIECES, SC_ROWS, LANE), jnp.int32)] + [pltpu.SemaphoreType.DMA] * SC_NBUF


def _gather(tab_hbm, idx_v, rows, sem, tl, ch, s):
    return pltpu.make_async_copy(tab_hbm.at[idx_v.at[tl, pl.ds(s * PEER_SEL + ch * SC_ROWS, SC_ROWS)]],
                                 rows.at[ch, s], sem)


def _widen_pair(packed_bf16):
    words = plsc.bitcast(packed_bf16, jnp.int32)
    return (plsc.bitcast(lax.shift_left(words, 16), F32), plsc.bitcast(words & HIGH_HALF, F32))


def _gelu_tanh_via_exp(x):
    z = 0.7978845608028654 * (x + 0.044715 * (x * x * x))
    return 0.5 * x * (2.0 - 2.0 / (jnp.exp(2.0 * z) + 1.0))


def _peer_experts_sc(u_pieces, v_pieces, xw, gates, piece_idx):
    n = xw.shape[0]
    per = n // SC_WORKERS
    d = PIECES * PACK_BLOCK

    @functools.partial(
        pl.kernel, mesh=_sc_mesh(),
        out_type=jax.ShapeDtypeStruct((n, d), F32),
        scratch_types=[pltpu.VMEM((SC_TB, PIECES * PEER_SEL), jnp.int32), pltpu.VMEM((SC_TB, PIECES * LANE), jnp.int32),
                       pltpu.VMEM((SC_TB, PEER_SEL), F32), pltpu.VMEM((SC_ROWS, SC_LANES), F32),
                       pltpu.VMEM((SC_TB, PEER_SEL), F32), pltpu.VMEM((SC_TB, PEER_SEL), jnp.int32),
                       pltpu.VMEM((SC_TB, d), F32)] + _sc_scratch() + [pltpu.SemaphoreType.DMA] * PIECES,
        compiler_params=pltpu.CompilerParams(needs_layout_passes=False),
        name="peer_experts_sc",
    )
    def run(u_hbm, v_hbm, x_hbm, g_hbm, idx_hbm, out_hbm, idx_v, x_v, res_v, part_v, g_v, w_v, acc_v,
            rows_all, *all_sems):
        sems, sems_v = all_sems[:SC_NBUF], all_sems[SC_NBUF:]
        base = _worker_id() * per
        lane = lax.iota(jnp.int32, SC_LANES)
        zero = jnp.zeros((SC_LANES,), F32)
        quads_per_chunk = SC_ROWS // SC_QUAD

        def gathers(tl, ch):
            return [_gather(u_hbm, idx_v, rows_all, sems[ch], tl, ch, s) for s in range(PIECES)]

        def gathers_v(tl, s):
            return [_gather(v_hbm, idx_v, rows_all, sems_v[s], tl, ch, s) for ch in range(SC_NBUF)]

        @pl.loop(0, per // SC_TB)
        def _(blk):
            tok0 = pl.multiple_of(base + blk * SC_TB, SC_TB)
            pltpu.sync_copy(idx_hbm.at[pl.ds(tok0, SC_TB)], idx_v)
            for ch in range(SC_NBUF):
                for cp in gathers(0, ch):
                    cp.start()
            pltpu.sync_copy(x_hbm.at[pl.ds(tok0, SC_TB)], x_v)
            pltpu.sync_copy(g_hbm.at[pl.ds(tok0, SC_TB)], g_v)

            @pl.loop(0, SC_TB)
            def _(tl):
                xs = [plsc.bitcast(x_v[tl, pl.ds(i * SC_LANES, SC_LANES)], BF16) for i in range(PIECES * SC_WORDS)]
                for ch in range(SC_NBUF):
                    for cp in gathers(tl, ch):
                        cp.wait()
                    rows = rows_all.at[ch]

                    @pl.loop(0, SC_ROWS, step=SC_GROUP)
                    def _(r0):
                        for r in range(SC_GROUP):
                            acc = zero
                            for s in range(PIECES):
                                for w0 in range(0, SC_WORDS, SC_QUAD):
                                    part = None
                                    for j in range(w0, w0 + SC_QUAD):
                                        prod = plsc.bitcast(rows[s, r0 + r, pl.ds(j * SC_LANES, SC_LANES)],
                                                            BF16) * xs[s * SC_WORDS + j]
                                        part = prod if part is None else part + prod
                                    lo, hi = _widen_pair(part)
                                    acc = acc + lo + hi
                            part_v[r0 + r, :] = acc
                    for g in range(SC_ROWS // SC_LANES):
                        row_ids = lane + g * SC_LANES
                        res = zero
                        for col in range(SC_LANES):
                            res = res + plsc.load_gather(part_v, [row_ids, jnp.full((SC_LANES,), col, jnp.int32)])
                        res_v[tl, pl.ds(ch * SC_ROWS + g * SC_LANES, SC_LANES)] = res

                    @pl.when(tl + 1 < SC_TB)
                    def _():
                        for cp in gathers(tl + 1, ch):
                            cp.start()

                    @pl.when(tl + 1 == SC_TB)
                    def _():
                        for s in range(PIECES):
                            _gather(v_hbm, idx_v, rows_all, sems_v[s], 0, ch, s).start()

            @pl.loop(0, SC_TB)
            def _(t):
                for c in range(0, PEER_SEL, SC_LANES):
                    w = g_v[t, pl.ds(c, SC_LANES)] * _gelu_tanh_via_exp(res_v[t, pl.ds(c, SC_LANES)])
                    w_v[t, pl.ds(c, SC_LANES)] = plsc.bitcast(plsc.pack(w, w, format=plsc.PackFormat.INTERLEAVED),
                                                             jnp.int32)

            @pl.loop(0, SC_TB)
            def _(tl):
                tl_vec = jnp.full((SC_LANES,), 0, jnp.int32) + tl
                for s in range(PIECES):
                    for cp in gathers_v(tl, s):
                        cp.wait()

                    def row_quad(q, accs):
                        ch = lax.shift_right_logical(q, quads_per_chunk.bit_length() - 1)
                        r0 = pl.multiple_of((q & (quads_per_chunk - 1)) * SC_QUAD, SC_QUAD)
                        quad = rows_all.at[ch, s, pl.ds(r0, SC_QUAD)]
                        col0 = jnp.full((SC_LANES,), 0, jnp.int32) + q * SC_QUAD
                        ws = [plsc.bitcast(plsc.load_gather(w_v, [tl_vec, col0 + j]), BF16) for j in range(SC_QUAD)]
                        accs = list(accs)
                        for k in range(SC_WORDS):
                            part = None
                            for j in range(SC_QUAD):
                                prod = plsc.bitcast(quad[j, pl.ds(k * SC_LANES, SC_LANES)], BF16) * ws[j]
                                part = prod if part is None else part + prod
                            lo, hi = _widen_pair(part)
                            accs[k] = accs[k] + lo
                            accs[SC_WORDS + k] = accs[SC_WORDS + k] + hi
                        return tuple(accs)

                    accs = lax.fori_loop(0, PEER_SEL // SC_QUAD, row_quad, (zero,) * (2 * SC_WORDS))
                    for i, acc in enumerate(accs):
                        acc_v[tl, pl.ds(s * PACK_BLOCK + i * SC_LANES, SC_LANES)] = acc

                    @pl.when(tl + 1 < SC_TB)
                    def _():
                        for cp in gathers_v(tl + 1, s):
                            cp.start()

            pltpu.sync_copy(acc_v, out_hbm.at[pl.ds(tok0, SC_TB)])

    return run(u_pieces, v_pieces, xw, gates, piece_idx)


FIN_TM = 1024


def _final_kernel(lat_ref, p_ref, gain_ref, g2_ref, *rest):
    o_ref = rest[-1]
    o_ref[...] = lat_ref[...] + g2_ref[0] * _rms(p_ref[...], gain_ref[...])


def _final(lat, peer, gain3, g2, out_all, row0, n_total, seq):
    n, d = lat.shape
    tps = seq // FIN_TM
    row = pl.BlockSpec((FIN_TM, d), lambda i: (i, 0))
    ins = [lat, peer, gain3, g2] + ([] if out_all is None else [out_all])
    return pl.pallas_call(
        _final_kernel, grid=(n // FIN_TM,),
        in_specs=[row, row, pl.BlockSpec((1, d), lambda i: (0, 0)),
                  pl.BlockSpec((1, 1, d), lambda i: (i // tps, 0, 0))]
        + ([] if out_all is None else [pl.BlockSpec(memory_space=pl.ANY)]),
        out_specs=pl.BlockSpec((FIN_TM, d), lambda i: (i + row0 // FIN_TM, 0)),
        out_shape=jax.ShapeDtypeStruct((n_total, d), F32),
        input_output_aliases={} if out_all is None else {len(ins) - 1: 0},
        compiler_params=_cparams("arbitrary"), name="final_residual",
    )(*ins)


PACK_TE = 512


def _pack_table_kernel(t_ref, o_ref):
    x = t_ref[...]
    for s in range(PIECES):
        b = s * PACK_BLOCK
        o_ref[pl.ds(s, PACK_TE, stride=PIECES), :] = _pack_bf16_pair(x[:, b:b + LANE], x[:, b + LANE:b + PACK_BLOCK])


def _expert_pieces(table):
    e, d = table.shape
    assert e % PACK_TE == 0 and d == PIECES * PACK_BLOCK
    return pl.pallas_call(
        _pack_table_kernel, grid=(e // PACK_TE,),
        in_specs=[pl.BlockSpec((PACK_TE, d), lambda i: (i, 0))],
        out_specs=pl.BlockSpec((PACK_TE * PIECES, LANE), lambda i: (i, 0)),
        out_shape=jax.ShapeDtypeStruct((e * PIECES, LANE), jnp.int32),
        compiler_params=_cparams("arbitrary"), name="pack_experts",
    )(table)


def _axial_rotary(seq):
    n_rows = seq // GRID_W
    rows = jnp.repeat(jnp.arange(n_rows, dtype=F32), GRID_W)
    cols = jnp.tile(jnp.arange(GRID_W, dtype=F32), n_rows)
    quarter = RET_DK // 4
    inv = ROPE_BASE ** (-jnp.arange(quarter, dtype=F32) / quarter)
    ang = jnp.concatenate([rows[:, None] * inv, cols[:, None] * inv], axis=-1)
    return jnp.cos(ang), jnp.sin(ang)


def kernel(x, c, ctx, c_ctx, w_mod, b_mod, norm_gain, w_in, ret_decay_raw, w_ret_o, conv_w, conv_b,
           conv_norm_g, conv_norm_b, w_conv_o, w_out, peer_wq, peer_keys, peer_u, peer_v):
    assert w_mod.shape[0] == 1, "single layer"
    batch, seq, d = x.shape
    n = batch * seq
    assert d == D_MODEL and seq % IN_TM == 0 and seq % (SC_WORKERS * SC_TB) == 0
    gain = norm_gain[0]
    x2 = x.reshape(n, d)

    pad = (-(batch + 1)) % 8
    c_all = jnp.concatenate([c, c_ctx[None, :], jnp.zeros((pad, d), F32)], axis=0)
    mod = _modulation(c_all, w_mod[0], b_mod[0][None, :])
    mod_lat = mod[:batch].reshape(batch, N_MOD, 1, d)
    sh1, sc1, g1, sh2, sc2, g2 = (mod_lat[:, i] for i in range(N_MOD))
    mod_ctx = mod[batch:batch + 1].reshape(N_MOD, 1, d)

    w_in0 = w_in[0]
    q_w = RET_HEADS * RET_DK
    w_kv = w_in0[:, q_w:3 * q_w].astype(BF16)
    st_f, st_b = _context_states(ret_decay_raw[0], ctx, gain[0:1], mod_ctx[0], mod_ctx[1], w_kv)

    hw = d // 2
    glu_a = w_in0[:, 4 * d:5 * d]
    glu_b = w_in0[:, 5 * d:6 * d]
    w_perm = jnp.concatenate([w_in0[:, :4 * d], glu_a[:, :hw], glu_b[:, :hw], glu_a[:, hw:], glu_b[:, hw:],
                              w_in0[:, 6 * d:]], axis=1).astype(BF16)
    cos, sin = _axial_rotary(seq)
    w_ret_b, w_conv_b, w_out_b, w_q_b = (w[0].astype(BF16) for w in (w_ret_o, w_conv_o, w_out, peer_wq))
    keys_b = peer_keys[0].astype(BF16)
    u_pieces = _expert_pieces(peer_u[0])
    v_pieces = _expert_pieces(peer_v[0])
    conv_vecs = (conv_b[0][None, :], conv_norm_g[0][None, :], conv_norm_b[0][None, :])

    groups = BATCH_GROUPS if sum(BATCH_GROUPS) == batch else (batch,)
    for tile_rows in (IN_TM, CONV_TL, MERGE_TM, ROUTE_T, FIN_TM, SC_WORKERS * SC_TB):
        assert seq % tile_rows == 0, "token count per sequence must be a multiple of every row tile"
    out = None
    b0 = 0
    after = jnp.zeros((8, PEER_SEL), F32)
    for gb in groups:
        bs = slice(b0, b0 + gb)
        row0 = b0 * seq
        xg = x2[row0:row0 + gb * seq]
        b0 += gb
        proj = _in_projection(xg, gain[0:1], sh1[bs], sc1[bs], cos, sin, w_perm, after, seq)
        y_ret = _retention(ret_decay_raw[0], st_f[bs], st_b[bs], proj, gb, seq)
        u_conv = _conv_branch(proj, conv_w[0], *conv_vecs, seq)
        lat1, h2_words, qp = _merge(xg, y_ret, u_conv, proj, w_ret_b, w_conv_b, w_out_b, w_q_b, gain[1:2],
                                    gain[2:3], g1[bs], sh2[bs], sc2[bs], seq)
        piece_idx, gates = _route(qp, keys_b)
        after = gates[:8]
        peer_out = _peer_experts_sc(u_pieces, v_pieces, h2_words, gates, piece_idx)
        out = _final(lat1, peer_out, gain[3:4], g2[bs], out, row0, n, seq)
    return out.reshape(batch, seq, d)
```

```python
import functools

import jax
import jax.numpy as jnp
from jax import lax
from jax.experimental import pallas as pl
from jax.experimental.pallas import tpu as pltpu
from jax.experimental.pallas import tpu_sc as plsc

F32 = jnp.float32
BF16 = jnp.bfloat16

D_MODEL = 1024
GRID_W = 64
EPS = 1e-6
N_MOD = 6
RET_HEADS = 4
RET_DK = 256
RET_CHUNK = 128
ROPE_BASE = 10000.0
CONV_WIDTH = 31
CONV_HALF = CONV_WIDTH // 2
PEER_HEADS = 8
PEER_N_KEYS = 128
PEER_DK_HALF = 128
PEER_TOPK = 16
PEER_SEL = PEER_HEADS * PEER_TOPK

SC_CORES = 2
SC_SUBCORES = 16
SC_WORKERS = SC_CORES * SC_SUBCORES
SC_LANES = 16
SC_ROWS = 32
SC_TB = 16
LANE = 128
PACK_BLOCK = 2 * LANE
PIECES = D_MODEL // PACK_BLOCK
HIGH_HALF = -65536

BATCH_GROUPS = (1, 1, 2, 3, 4, 5)

VMEM_LIMIT = 48 * 1024 * 1024


def _cparams(*sem):
    return pltpu.CompilerParams(dimension_semantics=sem, vmem_limit_bytes=VMEM_LIMIT)


def _sigmoid(x):
    return 1.0 / (1.0 + jnp.exp(-x))


def _silu(x):
    return x * _sigmoid(x)


def _softplus(x):
    return jnp.maximum(x, 0.0) + jnp.log1p(jnp.exp(-jnp.abs(x)))


def _rms(x, gain):
    return x * lax.rsqrt(jnp.mean(x * x, axis=-1, keepdims=True) + EPS) * gain


def _bf16_bits_high(x):
    return lax.bitcast_convert_type(x.astype(BF16).astype(F32), jnp.int32)


def _pack_bf16_pair(lo, hi):
    return _bf16_bits_high(hi) | lax.shift_right_logical(_bf16_bits_high(lo), 16)


def _pack_rows(x):
    parts = [_pack_bf16_pair(x[:, b:b + LANE], x[:, b + LANE:b + PACK_BLOCK])
             for b in range(0, x.shape[1], PACK_BLOCK)]
    return jnp.concatenate(parts, axis=1)


def _mod_kernel(c_ref, w_ref, b_ref, o_ref):
    a = _silu(c_ref[...])
    o_ref[...] = jnp.dot(a, w_ref[...], preferred_element_type=F32,
                         precision=lax.Precision.HIGHEST) + b_ref[...]


def _modulation(c_all, w_mod, b_mod):
    rows, d = c_all.shape
    n = w_mod.shape[1]
    return pl.pallas_call(
        _mod_kernel,
        grid=(n // d,),
        in_specs=[pl.BlockSpec((rows, d), lambda j: (0, 0)),
                  pl.BlockSpec((d, d), lambda j: (0, j)),
                  pl.BlockSpec((1, d), lambda j: (0, j))],
        out_specs=pl.BlockSpec((rows, d), lambda j: (0, j)),
        out_shape=jax.ShapeDtypeStruct((rows, n), F32),
        compiler_params=_cparams("arbitrary"),
        name="modulation",
    )(c_all, w_mod, b_mod)


def _ctx_kernel(raw_ref, ctx_ref, gain_ref, sh_ref, sc_ref, wkv_ref, sf_ref, sb_ref):
    x = ctx_ref[0]
    lc = x.shape[0]
    hc = _rms(x, gain_ref[...]) * (1.0 + sc_ref[...]) + sh_ref[...]
    kv = jnp.dot(hc.astype(BF16), wkv_ref[...], preferred_element_type=F32)
    pos = lax.broadcasted_iota(jnp.int32, (lc, RET_DK), 0).astype(F32)
    tn = (((0,), (0,)), ((), ()))
    for h in range(RET_HEADS):
        lgf = -_softplus(jnp.full((lc, RET_DK), raw_ref[0, h], F32))
        lgb = -_softplus(jnp.full((lc, RET_DK), raw_ref[1, h], F32))
        k = kv[:, h * RET_DK:(h + 1) * RET_DK] * (RET_DK ** -0.5)
        v = kv[:, (RET_HEADS + h) * RET_DK:(RET_HEADS + h + 1) * RET_DK].astype(BF16)
        kf = (k * jnp.exp(lgf * (lc - 1.0 - pos))).astype(BF16)
        kb = (k * jnp.exp(lgb * pos)).astype(BF16)
        sf_ref[0, h] = lax.dot_general(kf, v, tn, preferred_element_type=F32)
        sb_ref[0, h] = lax.dot_general(kb, v, tn, preferred_element_type=F32)


def _context_states(raw, ctx, gain0, csh, csc, w_kv):
    b, lc, d = ctx.shape
    st = jax.ShapeDtypeStruct((b, RET_HEADS, RET_DK, RET_DK), F32)
    st_spec = pl.BlockSpec((1, RET_HEADS, RET_DK, RET_DK), lambda i: (i, 0, 0, 0))
    vec = pl.BlockSpec((1, d), lambda i: (0, 0))
    return pl.pallas_call(
        _ctx_kernel,
        grid=(b,),
        in_specs=[pl.BlockSpec(memory_space=pltpu.SMEM),
                  pl.BlockSpec((1, lc, d), lambda i: (i, 0, 0)),
                  vec, vec, vec,
                  pl.BlockSpec(w_kv.shape, lambda i: (0, 0))],
        out_specs=[st_spec, st_spec],
        out_shape=[st, st],
        compiler_params=_cparams("arbitrary"),
        name="context_states",
    )(raw, ctx, gain0, csh, csc, w_kv)


IN_TM = 1024
IN_TILES = 8
IN_OUT_TILES = 7


def _inproj_kernel(x_ref, gain_ref, sh_ref, sc_ref, cos_ref, sin_ref, w_ref, after_ref, o_ref, xn_ref):
    del after_ref
    j = pl.program_id(1)

    @pl.when(j == 0)
    def _():
        h = _rms(x_ref[...], gain_ref[...]) * (1.0 + sc_ref[0]) + sh_ref[0]
        xn_ref[...] = h.astype(BF16)

    acc = jnp.dot(xn_ref[...], w_ref[...], preferred_element_type=F32)
    half = RET_DK // 2

    def rotary(scale):
        cos = cos_ref[...]
        sin = sin_ref[...]
        for h in range(RET_HEADS):
            t1 = acc[:, h * RET_DK:h * RET_DK + half]
            t2 = acc[:, h * RET_DK + half:(h + 1) * RET_DK]
            o_ref[:, h * RET_DK:h * RET_DK + half] = ((t1 * cos - t2 * sin) * scale).astype(BF16)
            o_ref[:, h * RET_DK + half:(h + 1) * RET_DK] = ((t1 * sin + t2 * cos) * scale).astype(BF16)

    @pl.when(j == 0)
    def _():
        rotary(1.0)

    @pl.when(j == 1)
    def _():
        rotary(RET_DK ** -0.5)

    @pl.when(j == 2)
    def _():
        o_ref[...] = acc.astype(BF16)

    @pl.when(j == 3)
    def _():
        o_ref[...] = _silu(acc).astype(BF16)

    hw = D_MODEL // 2

    @pl.when(j == 4)
    def _():
        o_ref[:, :hw] = (acc[:, :hw] * _sigmoid(acc[:, hw:])).astype(BF16)

    @pl.when(j == 5)
    def _():
        o_ref[:, hw:] = (acc[:, :hw] * _sigmoid(acc[:, hw:])).astype(BF16)

    @pl.when(j >= 6)
    def _():
        o_ref[...] = _sigmoid(acc).astype(BF16)


def _out_tile(j):
    return jnp.where(j <= 4, j, j - 1)


def _in_projection(x2, gain0, sh1, sc1, cos, sin, w_perm, after, seq):
    n, d = x2.shape
    tiles_per_seq = seq // IN_TM
    return pl.pallas_call(
        _inproj_kernel,
        grid=(n // IN_TM, IN_TILES),
        in_specs=[pl.BlockSpec((IN_TM, d), lambda i, j: (i, 0)),
                  pl.BlockSpec((1, d), lambda i, j: (0, 0)),
                  pl.BlockSpec((1, 1, d), lambda i, j: (i // tiles_per_seq, 0, 0)),
                  pl.BlockSpec((1, 1, d), lambda i, j: (i // tiles_per_seq, 0, 0)),
                  pl.BlockSpec((IN_TM, RET_DK // 2), lambda i, j: (i % tiles_per_seq, 0)),
                  pl.BlockSpec((IN_TM, RET_DK // 2), lambda i, j: (i % tiles_per_seq, 0)),
                  pl.BlockSpec((d, d), lambda i, j: (0, j)),
                  pl.BlockSpec(after.shape, lambda i, j: (0, 0))],
        out_specs=pl.BlockSpec((IN_TM, d), lambda i, j: (i, _out_tile(j))),
        out_shape=jax.ShapeDtypeStruct((n, IN_OUT_TILES * d), BF16),
        scratch_shapes=[pltpu.VMEM((IN_TM, d), BF16)],
        compiler_params=_cparams("arbitrary", "arbitrary"),
        name="in_projection",
    )(x2, gain0, sh1, sc1, cos, sin, w_perm, after)


def _ret_kernel(raw_ref, s0f_ref, s0b_ref, q_ref, k_ref, v_ref, g_ref, o_ref, sf_ref, sb_ref, y_ref):
    hd = pl.program_id(1)
    c = RET_CHUNK
    seq = q_ref.shape[0]
    nc = seq // c

    def log_gamma(direction, shape):
        return -_softplus(jnp.full(shape, raw_ref[direction, hd], F32))

    ri = lax.broadcasted_iota(jnp.int32, (c, c), 0).astype(F32)
    ci = lax.broadcasted_iota(jnp.int32, (c, c), 1).astype(F32)
    dec_f = jnp.where(ri >= ci, jnp.exp(log_gamma(0, (c, c)) * jnp.maximum(ri - ci, 0.0)), 0.0)
    dec_b = jnp.where(ci >= ri, jnp.exp(log_gamma(1, (c, c)) * jnp.maximum(ci - ri, 0.0)), 0.0)
    pos = lax.broadcasted_iota(jnp.int32, (c, RET_DK), 0).astype(F32)
    lgf = log_gamma(0, (c, RET_DK))
    lgb = log_gamma(1, (c, RET_DK))
    qdec_f = jnp.exp(lgf * (pos + 1.0))
    kdec_f = jnp.exp(lgf * (c - 1.0 - pos))
    qdec_b = jnp.exp(lgb * (c - pos))
    kdec_b = jnp.exp(lgb * pos)
    cdec_f = jnp.exp(log_gamma(0, (1, RET_DK)) * c)
    cdec_b = jnp.exp(log_gamma(1, (1, RET_DK)) * c)
    sf_ref[...] = s0f_ref[0, 0]
    sb_ref[...] = s0b_ref[0, 0]
    nt = (((1,), (1,)), ((), ()))
    tn = (((0,), (0,)), ((), ()))

    def chunk(st_ref, row, dec, qdec, kdec, cdec):
        q = q_ref[pl.ds(row, c), :]
        k = k_ref[pl.ds(row, c), :]
        v = v_ref[pl.ds(row, c), :]
        s = lax.dot_general(q, k, nt, preferred_element_type=F32) * dec
        st = st_ref[...]
        y = jnp.dot(s.astype(BF16), v, preferred_element_type=F32)
        y = y + jnp.dot((q.astype(F32) * qdec).astype(BF16), st.astype(BF16), preferred_element_type=F32)
        kd = (k.astype(F32) * kdec).astype(BF16)
        st_ref[...] = st * cdec + lax.dot_general(kd, v, tn, preferred_element_type=F32)
        return y

    def first_half(i, carry):
        rf = pl.multiple_of(i * c, c)
        rb = pl.multiple_of((nc - 1 - i) * c, c)
        y_ref[pl.ds(rf, c), :] = chunk(sf_ref, rf, dec_f, qdec_f, kdec_f, cdec_f)
        y_ref[pl.ds(rb, c), :] = chunk(sb_ref, rb, dec_b, qdec_b, kdec_b, cdec_b)
        return carry

    def second_half(i, carry):
        rf = pl.multiple_of(i * c, c)
        rb = pl.multiple_of((nc - 1 - i) * c, c)
        y_ref[pl.ds(rf, c), :] += chunk(sf_ref, rf, dec_f, qdec_f, kdec_f, cdec_f)
        y_ref[pl.ds(rb, c), :] += chunk(sb_ref, rb, dec_b, qdec_b, kdec_b, cdec_b)
        return carry

    lax.fori_loop(0, nc // 2, first_half, 0)
    lax.fori_loop(nc // 2, nc, second_half, 0)

    def finish(i, carry):
        r = pl.multiple_of(i * c, c)
        y = y_ref[pl.ds(r, c), :]
        yn = y * lax.rsqrt(jnp.mean(y * y, axis=-1, keepdims=True) + EPS)
        o_ref[pl.ds(r, c), :] = (yn * g_ref[pl.ds(r, c), :].astype(F32)).astype(BF16)
        return carry

    lax.fori_loop(0, nc, finish, 0)


def _retention(raw, st_f, st_b, proj, batch, seq):
    n = batch * seq
    st_spec = pl.BlockSpec((1, 1, RET_DK, RET_DK), lambda b, h: (b, h, 0, 0))

    def col(tile):
        return pl.BlockSpec((seq, RET_DK), lambda b, h: (b, tile * RET_HEADS + h))

    return pl.pallas_call(
        _ret_kernel,
        grid=(batch, RET_HEADS),
        in_specs=[pl.BlockSpec(memory_space=pltpu.SMEM), st_spec, st_spec,
                  col(0), col(1), col(2), col(3)],
        out_specs=pl.BlockSpec((seq, RET_DK), lambda b, h: (b, h)),
        out_shape=jax.ShapeDtypeStruct((n, RET_HEADS * RET_DK), BF16),
        scratch_shapes=[pltpu.VMEM((RET_DK, RET_DK), F32), pltpu.VMEM((RET_DK, RET_DK), F32),
                        pltpu.VMEM((seq, RET_DK), F32)],
        compiler_params=_cparams("arbitrary", "arbitrary"),
        name="retention",
    )(raw, st_f, st_b, proj, proj, proj, proj)


CONV_TL = 256
CONV_HALO = 16
CONV_RB = 32


def _conv_kernel(prev_ref, cur_ref, next_ref, cw_ref, cb_ref, lg_ref, lb_ref, o_ref, pad_ref, *, seq):
    i = pl.program_id(0)
    tl = CONV_TL
    has_prev = (i * tl) % seq != 0
    has_next = ((i + 1) * tl) % seq != 0
    pad_ref[0:CONV_HALO, :] = jnp.where(has_prev, prev_ref[...].astype(F32), 0.0)
    pad_ref[CONV_HALO:CONV_HALO + tl, :] = cur_ref[...].astype(F32)
    pad_ref[CONV_HALO + tl:, :] = jnp.where(has_next, next_ref[...].astype(F32), 0.0)
    off = CONV_HALO - CONV_HALF
    for rb in range(tl // CONV_RB):
        base = rb * CONV_RB + off
        acc = pad_ref[base:base + CONV_RB, :] * cw_ref[0:1, :]
        for w in range(1, CONV_WIDTH):
            acc = acc + pad_ref[base + w:base + w + CONV_RB, :] * cw_ref[w:w + 1, :]
        u = acc + cb_ref[...]
        mu = jnp.mean(u, axis=-1, keepdims=True)
        uc = u - mu
        y = uc * lax.rsqrt(jnp.mean(uc * uc, axis=-1, keepdims=True) + EPS) * lg_ref[...] + lb_ref[...]
        o_ref[rb * CONV_RB:(rb + 1) * CONV_RB, :] = _silu(y).astype(BF16)


def _conv_branch(proj, conv_w, conv_b, ln_g, ln_b, seq):
    n = proj.shape[0]
    d = D_MODEL
    glu_tile = 4
    hb = CONV_TL // CONV_HALO
    n_halo = n // CONV_HALO
    vec = pl.BlockSpec((1, d), lambda i: (0, 0))
    return pl.pallas_call(
        functools.partial(_conv_kernel, seq=seq),
        grid=(n // CONV_TL,),
        in_specs=[pl.BlockSpec((CONV_HALO, d), lambda i: (jnp.maximum(i * hb - 1, 0), glu_tile)),
                  pl.BlockSpec((CONV_TL, d), lambda i: (i, glu_tile)),
                  pl.BlockSpec((CONV_HALO, d), lambda i: (jnp.minimum((i + 1) * hb, n_halo - 1), glu_tile)),
                  pl.BlockSpec((CONV_WIDTH, d), lambda i: (0, 0)),
                  vec, vec, vec],
        out_specs=pl.BlockSpec((CONV_TL, d), lambda i: (i, 0)),
        out_shape=jax.ShapeDtypeStruct((n, d), BF16),
        scratch_shapes=[pltpu.VMEM((CONV_TL + 2 * CONV_HALO, d), F32)],
        compiler_params=_cparams("arbitrary"),
        name="conv_branch",
    )(proj, proj, proj, conv_w, conv_b, ln_g, ln_b)


MERGE_TM = 256


def _merge_kernel(x_ref, yr_ref, uc_ref, ga_ref, gb_ref, wr_ref, wc_ref, wo_ref, wq_ref,
                  gain1_ref, gain2_ref, g1_ref, sh2_ref, sc2_ref, lat_ref, h2_ref, qp_ref):
    a = jnp.dot(yr_ref[...], wr_ref[...], preferred_element_type=F32)
    b = jnp.dot(uc_ref[...], wc_ref[...], preferred_element_type=F32)
    merged = ga_ref[...].astype(F32) * a + gb_ref[...].astype(F32) * b
    y = jnp.dot(merged.astype(BF16), wo_ref[...], preferred_element_type=F32)
    lat = x_ref[...] + g1_ref[0] * _rms(y, gain1_ref[...])
    lat_ref[...] = lat
    h2 = _rms(lat, gain2_ref[...]) * (1.0 + sc2_ref[0]) + sh2_ref[0]
    h2_ref[...] = _pack_rows(h2)
    qp_ref[...] = jnp.dot(h2.astype(BF16), wq_ref[...], preferred_element_type=F32).astype(BF16)


def _merge(x2, y_ret, u_conv, proj, w_ret_o, w_conv_o, w_out, w_q, gain1, gain2, g1, sh2, sc2, seq):
    n, d = x2.shape
    tps = seq // MERGE_TM
    nq = w_q.shape[1]
    row = pl.BlockSpec((MERGE_TM, d), lambda i: (i, 0))
    wsp = pl.BlockSpec((d, d), lambda i: (0, 0))
    vec = pl.BlockSpec((1, d), lambda i: (0, 0))
    bvec = pl.BlockSpec((1, 1, d), lambda i: (i // tps, 0, 0))
    return pl.pallas_call(
        _merge_kernel,
        grid=(n // MERGE_TM,),
        in_specs=[row, row, row,
                  pl.BlockSpec((MERGE_TM, d), lambda i: (i, 5)),
                  pl.BlockSpec((MERGE_TM, d), lambda i: (i, 6)),
                  wsp, wsp, wsp,
                  pl.BlockSpec((d, nq), lambda i: (0, 0)),
                  vec, vec, bvec, bvec, bvec],
        out_specs=[row, pl.BlockSpec((MERGE_TM, d // 2), lambda i: (i, 0)),
                   pl.BlockSpec((MERGE_TM, nq), lambda i: (i, 0))],
        out_shape=[jax.ShapeDtypeStruct((n, d), F32), jax.ShapeDtypeStruct((n, d // 2), jnp.int32),
                   jax.ShapeDtypeStruct((n, nq), BF16)],
        compiler_params=_cparams("arbitrary"),
        name="merge",
    )(x2, y_ret, u_conv, proj, proj, w_ret_o, w_conv_o, w_out, w_q, gain1, gain2, g1, sh2, sc2)


ROUTE_T = 512


def _topk_rows(s, k, payload=None):
    r = s.shape[0]
    rows = lax.broadcasted_iota(jnp.int32, s.shape, 0).astype(F32)
    vals, idxs = [], []
    for _ in range(k):
        m = jnp.max(s, axis=0, keepdims=True)
        pos = jnp.min(jnp.where(s == m, rows, float(r)), axis=0, keepdims=True)
        hit = rows == pos
        vals.append(m)
        if payload is None:
            idxs.append(pos)
        else:
            idxs.append(jnp.max(jnp.where(hit, payload, -1.0), axis=0, keepdims=True))
        s = jnp.where(hit, -jnp.inf, s)
    return jnp.concatenate(vals, axis=0), jnp.concatenate(idxs, axis=0)


def _route_kernel(q_ref, keys_ref, e_ref, g_ref, es_ref, gs_ref):
    h = pl.program_id(1)
    nt = (((1,), (1,)), ((), ()))
    row0 = pl.multiple_of(h * PEER_TOPK, PEER_TOPK)
    for t in range(ROUTE_T // LANE):
        cols = slice(t * LANE, (t + 1) * LANE)
        tops = []
        for p in range(2):
            qh = q_ref[cols, p * PEER_DK_HALF:(p + 1) * PEER_DK_HALF]
            s = lax.dot_general(keys_ref[0, p], qh, nt, preferred_element_type=F32)
            tops.append(_topk_rows(s, PEER_TOPK))
        (v1, i1), (v2, i2) = tops
        widths = [PEER_TOPK // (a + 1) for a in range(PEER_TOPK)]
        pad = (-sum(widths)) % 8
        cand = jnp.concatenate([v1[a:a + 1] + v2[:w] for a, w in enumerate(widths)]
                               + [jnp.full((pad, LANE), -jnp.inf, F32)], axis=0)
        cidx = jnp.concatenate([i1[a:a + 1] * float(PEER_N_KEYS) + i2[:w] for a, w in enumerate(widths)]
                               + [jnp.zeros((pad, LANE), F32)], axis=0)
        best, experts = _topk_rows(cand, PEER_TOPK, payload=cidx)
        ex = jnp.exp(best - best[0:1])
        gates = ex / jnp.sum(ex, axis=0, keepdims=True)
        es_ref[pl.ds(row0, PEER_TOPK), cols] = experts.astype(jnp.int32)
        gs_ref[pl.ds(row0, PEER_TOPK), cols] = gates

    @pl.when(h == PEER_HEADS - 1)
    def _():
        first_piece = es_ref[...].T * PIECES
        for s in range(PIECES):
            e_ref[:, s * PEER_SEL:(s + 1) * PEER_SEL] = first_piece + s
        g_ref[...] = gs_ref[...].T


def _route(qp, keys):
    n = qp.shape[0]
    hw = 2 * PEER_DK_HALF
    out = pl.BlockSpec((ROUTE_T, PEER_SEL), lambda i, h: (i, 0))
    return pl.pallas_call(
        _route_kernel,
        grid=(n // ROUTE_T, PEER_HEADS),
        in_specs=[pl.BlockSpec((ROUTE_T, hw), lambda i, h: (i, h)),
                  pl.BlockSpec((1, 2, PEER_N_KEYS, PEER_DK_HALF), lambda i, h: (h, 0, 0, 0))],
        out_specs=[pl.BlockSpec((ROUTE_T, PIECES * PEER_SEL), lambda i, h: (i, 0)), out],
        out_shape=[jax.ShapeDtypeStruct((n, PIECES * PEER_SEL), jnp.int32),
                   jax.ShapeDtypeStruct((n, PEER_SEL), F32)],
        scratch_shapes=[pltpu.VMEM((PEER_SEL, ROUTE_T), jnp.int32), pltpu.VMEM((PEER_SEL, ROUTE_T), F32)],
        compiler_params=_cparams("arbitrary", "arbitrary"),
        name="peer_route",
    )(qp, keys)


def _sc_mesh():
    return plsc.VectorSubcoreMesh(core_axis_name="c", subcore_axis_name="s")


def _worker_id():
    return lax.axis_index("s") * SC_CORES + lax.axis_index("c")


SC_NBUF = PEER_SEL // SC_ROWS
SC_QUAD = 4
SC_GROUP = 2
SC_WORDS = LANE // SC_LANES


def _sc_scratch():
    assert SC_NBUF == PIECES
    return [pltpu.VMEM((SC_NBUF, PIECES, SC_ROWS, LANE), jnp.int32)] + [pltpu.SemaphoreType.DMA] * SC_NBUF


def _gather(tab_hbm, idx_v, rows, sem, tl, ch, s):
    return pltpu.make_async_copy(tab_hbm.at[idx_v.at[tl, pl.ds(s * PEER_SEL + ch * SC_ROWS, SC_ROWS)]],
                                 rows.at[ch, s], sem)


def _widen_pair(packed_bf16):
    words = plsc.bitcast(packed_bf16, jnp.int32)
    return (plsc.bitcast(lax.shift_left(words, 16), F32), plsc.bitcast(words & HIGH_HALF, F32))


def _gelu_tanh_via_exp(x):
    z = 0.7978845608028654 * (x + 0.044715 * (x * x * x))
    return 0.5 * x * (2.0 - 2.0 / (jnp.exp(2.0 * z) + 1.0))


def _peer_experts_sc(u_pieces, v_pieces, xw, gates, piece_idx):
    n = xw.shape[0]
    per = n // SC_WORKERS
    d = PIECES * PACK_BLOCK

    @functools.partial(
        pl.kernel, mesh=_sc_mesh(),
        out_type=jax.ShapeDtypeStruct((n, d), F32),
        scratch_types=[pltpu.VMEM((2, SC_TB, PIECES * PEER_SEL), jnp.int32),
                       pltpu.VMEM((SC_TB, PIECES * LANE), jnp.int32),
                       pltpu.VMEM((SC_TB, PEER_SEL), F32), pltpu.VMEM((SC_ROWS, SC_LANES), F32),
                       pltpu.VMEM((SC_TB, PEER_SEL), F32), pltpu.VMEM((SC_TB, PEER_SEL), jnp.int32),
                       pltpu.VMEM((SC_TB, d), F32)] + _sc_scratch() + [pltpu.SemaphoreType.DMA] * PIECES,
        compiler_params=pltpu.CompilerParams(needs_layout_passes=False),
        name="peer_experts_sc",
    )
    def run(u_hbm, v_hbm, x_hbm, g_hbm, idx_hbm, out_hbm, idx2_v, x_v, res_v, part_v, g_v, w_v, acc_v,
            rows_all, *all_sems):
        sems, sems_v = all_sems[:SC_NBUF], all_sems[SC_NBUF:]
        base = _worker_id() * per
        nblk = per // SC_TB
        lane = lax.iota(jnp.int32, SC_LANES)
        zero = jnp.zeros((SC_LANES,), F32)
        quads_per_chunk = SC_ROWS // SC_QUAD

        def first_score_gathers(idx_v, s):
            return [_gather(u_hbm, idx_v, rows_all, sems[ch], 0, ch, s) for ch in range(SC_NBUF)]

        pltpu.sync_copy(idx_hbm.at[pl.ds(base, SC_TB)], idx2_v.at[0])
        for s in range(PIECES):
            for cp in first_score_gathers(idx2_v.at[0], s):
                cp.start()

        @pl.loop(0, nblk)
        def _(blk):
            tok0 = pl.multiple_of(base + blk * SC_TB, SC_TB)
            idx_v = idx2_v.at[blk & 1]
            idx_next = idx2_v.at[1 - (blk & 1)]
            has_next = blk + 1 < nblk

            def gathers(tl, ch):
                return [_gather(u_hbm, idx_v, rows_all, sems[ch], tl, ch, s) for s in range(PIECES)]

            def gathers_v(tl, s):
                return [_gather(v_hbm, idx_v, rows_all, sems_v[s], tl, ch, s) for ch in range(SC_NBUF)]

            pltpu.sync_copy(x_hbm.at[pl.ds(tok0, SC_TB)], x_v)
            pltpu.sync_copy(g_hbm.at[pl.ds(tok0, SC_TB)], g_v)

            @pl.when(has_next)
            def _():
                pltpu.sync_copy(idx_hbm.at[pl.ds(tok0 + SC_TB, SC_TB)], idx_next)

            @pl.loop(0, SC_TB)
            def _(tl):
                xs = [plsc.bitcast(x_v[tl, pl.ds(i * SC_LANES, SC_LANES)], BF16) for i in range(PIECES * SC_WORDS)]
                for ch in range(SC_NBUF):
                    for cp in gathers(tl, ch):
                        cp.wait()
                    rows = rows_all.at[ch]

                    @pl.loop(0, SC_ROWS, step=SC_GROUP)
                    def _(r0):
                        for r in range(SC_GROUP):
                            acc = zero
                            for s in range(PIECES):
                                for w0 in range(0, SC_WORDS, SC_QUAD):
                                    part = None
                                    for j in range(w0, w0 + SC_QUAD):
                                        prod = plsc.bitcast(rows[s, r0 + r, pl.ds(j * SC_LANES, SC_LANES)],
                                                            BF16) * xs[s * SC_WORDS + j]
                                        part = prod if part is None else part + prod
                                    lo, hi = _widen_pair(part)
                                    acc = acc + lo + hi
                            part_v[r0 + r, :] = acc
                    for g in range(SC_ROWS // SC_LANES):
                        row_ids = lane + g * SC_LANES
                        res = zero
                        for col in range(SC_LANES):
                            res = res + plsc.load_gather(part_v, [row_ids, jnp.full((SC_LANES,), col, jnp.int32)])
                        res_v[tl, pl.ds(ch * SC_ROWS + g * SC_LANES, SC_LANES)] = res

                    @pl.when(tl + 1 < SC_TB)
                    def _():
                        for cp in gathers(tl + 1, ch):
                            cp.start()

                    @pl.when(tl + 1 == SC_TB)
                    def _():
                        for s in range(PIECES):
                            _gather(v_hbm, idx_v, rows_all, sems_v[s], 0, ch, s).start()

            @pl.loop(0, SC_TB)
            def _(t):
                for c in range(0, PEER_SEL, SC_LANES):
                    w = g_v[t, pl.ds(c, SC_LANES)] * _gelu_tanh_via_exp(res_v[t, pl.ds(c, SC_LANES)])
                    w_v[t, pl.ds(c, SC_LANES)] = plsc.bitcast(plsc.pack(w, w, format=plsc.PackFormat.INTERLEAVED),
                                                             jnp.int32)

            @pl.loop(0, SC_TB)
            def _(tl):
                tl_vec = jnp.full((SC_LANES,), 0, jnp.int32) + tl
                for s in range(PIECES):
                    for cp in gathers_v(tl, s):
                        cp.wait()

                    def row_quad(q, accs):
                        ch = lax.shift_right_logical(q, quads_per_chunk.bit_length() - 1)
                        r0 = pl.multiple_of((q & (quads_per_chunk - 1)) * SC_QUAD, SC_QUAD)
                        quad = rows_all.at[ch, s, pl.ds(r0, SC_QUAD)]
                        col0 = jnp.full((SC_LANES,), 0, jnp.int32) + q * SC_QUAD
                        ws = [plsc.bitcast(plsc.load_gather(w_v, [tl_vec, col0 + j]), BF16) for j in range(SC_QUAD)]
                        accs = list(accs)
                        for k in range(SC_WORDS):
                            part = None
                            for j in range(SC_QUAD):
                                prod = plsc.bitcast(quad[j, pl.ds(k * SC_LANES, SC_LANES)], BF16) * ws[j]
                                part = prod if part is None else part + prod
                            lo, hi = _widen_pair(part)
                            accs[k] = accs[k] + lo
                            accs[SC_WORDS + k] = accs[SC_WORDS + k] + hi
                        return tuple(accs)

                    accs = lax.fori_loop(0, PEER_SEL // SC_QUAD, row_quad, (zero,) * (2 * SC_WORDS))
                    for i, acc in enumerate(accs):
                        acc_v[tl, pl.ds(s * PACK_BLOCK + i * SC_LANES, SC_LANES)] = acc

                    @pl.when(tl + 1 < SC_TB)
                    def _():
                        for cp in gathers_v(tl + 1, s):
                            cp.start()

                    @pl.when(jnp.logical_and(tl + 1 == SC_TB, has_next))
                    def _():
                        for cp in first_score_gathers(idx_next, s):
                            cp.start()

            pltpu.sync_copy(acc_v, out_hbm.at[pl.ds(tok0, SC_TB)])

    return run(u_pieces, v_pieces, xw, gates, piece_idx)


FIN_TM = 1024


def _final_kernel(lat_ref, p_ref, gain_ref, g2_ref, *rest):
    o_ref = rest[-1]
    o_ref[...] = lat_ref[...] + g2_ref[0] * _rms(p_ref[...], gain_ref[...])


def _final(lat, peer, gain3, g2, out_all, row0, n_total, seq):
    n, d = lat.shape
    tps = seq // FIN_TM
    row = pl.BlockSpec((FIN_TM, d), lambda i: (i, 0))
    ins = [lat, peer, gain3, g2] + ([] if out_all is None else [out_all])
    return pl.pallas_call(
        _final_kernel, grid=(n // FIN_TM,),
        in_specs=[row, row, pl.BlockSpec((1, d), lambda i: (0, 0)),
                  pl.BlockSpec((1, 1, d), lambda i: (i // tps, 0, 0))]
        + ([] if out_all is None else [pl.BlockSpec(memory_space=pl.ANY)]),
        out_specs=pl.BlockSpec((FIN_TM, d), lambda i: (i + row0 // FIN_TM, 0)),
        out_shape=jax.ShapeDtypeStruct((n_total, d), F32),
        input_output_aliases={} if out_all is None else {len(ins) - 1: 0},
        compiler_params=_cparams("arbitrary"), name="final_residual",
    )(*ins)


PACK_TE = 512


def _pack_table_kernel(t_ref, o_ref):
    x = t_ref[...]
    for s in range(PIECES):
        b = s * PACK_BLOCK
        o_ref[pl.ds(s, PACK_TE, stride=PIECES), :] = _pack_bf16_pair(x[:, b:b + LANE], x[:, b + LANE:b + PACK_BLOCK])


def _expert_pieces(table):
    e, d = table.shape
    assert e % PACK_TE == 0 and d == PIECES * PACK_BLOCK
    return pl.pallas_call(
        _pack_table_kernel, grid=(e // PACK_TE,),
        in_specs=[pl.BlockSpec((PACK_TE, d), lambda i: (i, 0))],
        out_specs=pl.BlockSpec((PACK_TE * PIECES, LANE), lambda i: (i, 0)),
        out_shape=jax.ShapeDtypeStruct((e * PIECES, LANE), jnp.int32),
        compiler_params=_cparams("arbitrary"), name="pack_experts",
    )(table)


def _axial_rotary(seq):
    n_rows = seq // GRID_W
    rows = jnp.repeat(jnp.arange(n_rows, dtype=F32), GRID_W)
    cols = jnp.tile(jnp.arange(GRID_W, dtype=F32), n_rows)
    quarter = RET_DK // 4
    inv = ROPE_BASE ** (-jnp.arange(quarter, dtype=F32) / quarter)
    ang = jnp.concatenate([rows[:, None] * inv, cols[:, None] * inv], axis=-1)
    return jnp.cos(ang), jnp.sin(ang)


def kernel(x, c, ctx, c_ctx, w_mod, b_mod, norm_gain, w_in, ret_decay_raw, w_ret_o, conv_w, conv_b,
           conv_norm_g, conv_norm_b, w_conv_o, w_out, peer_wq, peer_keys, peer_u, peer_v):
    assert w_mod.shape[0] == 1, "single layer"
    batch, seq, d = x.shape
    n = batch * seq
    assert d == D_MODEL and seq % IN_TM == 0 and seq % (SC_WORKERS * SC_TB) == 0
    gain = norm_gain[0]
    x2 = x.reshape(n, d)

    pad = (-(batch + 1)) % 8
    c_all = jnp.concatenate([c, c_ctx[None, :], jnp.zeros((pad, d), F32)], axis=0)
    mod = _modulation(c_all, w_mod[0], b_mod[0][None, :])
    mod_lat = mod[:batch].reshape(batch, N_MOD, 1, d)
    sh1, sc1, g1, sh2, sc2, g2 = (mod_lat[:, i] for i in range(N_MOD))
    mod_ctx = mod[batch:batch + 1].reshape(N_MOD, 1, d)

    w_in0 = w_in[0]
    q_w = RET_HEADS * RET_DK
    w_kv = w_in0[:, q_w:3 * q_w].astype(BF16)
    st_f, st_b = _context_states(ret_decay_raw[0], ctx, gain[0:1], mod_ctx[0], mod_ctx[1], w_kv)

    hw = d // 2
    glu_a = w_in0[:, 4 * d:5 * d]
    glu_b = w_in0[:, 5 * d:6 * d]
    w_perm = jnp.concatenate([w_in0[:, :4 * d], glu_a[:, :hw], glu_b[:, :hw], glu_a[:, hw:], glu_b[:, hw:],
                              w_in0[:, 6 * d:]], axis=1).astype(BF16)
    cos, sin = _axial_rotary(seq)
    w_ret_b, w_conv_b, w_out_b, w_q_b = (w[0].astype(BF16) for w in (w_ret_o, w_conv_o, w_out, peer_wq))
    keys_b = peer_keys[0].astype(BF16)
    u_pieces = _expert_pieces(peer_u[0])
    v_pieces = _expert_pieces(peer_v[0])
    conv_vecs = (conv_b[0][None, :], conv_norm_g[0][None, :], conv_norm_b[0][None, :])

    groups = BATCH_GROUPS if sum(BATCH_GROUPS) == batch else (batch,)
    for tile_rows in (IN_TM, CONV_TL, MERGE_TM, ROUTE_T, FIN_TM, SC_WORKERS * SC_TB):
        assert seq % tile_rows == 0, "token count per sequence must be a multiple of every row tile"
    out = None
    b0 = 0
    after = jnp.zeros((8, PEER_SEL), F32)
    for gb in groups:
        bs = slice(b0, b0 + gb)
        row0 = b0 * seq
        xg = x2[row0:row0 + gb * seq]
        b0 += gb
        proj = _in_projection(xg, gain[0:1], sh1[bs], sc1[bs], cos, sin, w_perm, after, seq)
        y_ret = _retention(ret_decay_raw[0], st_f[bs], st_b[bs], proj, gb, seq)
        u_conv = _conv_branch(proj, conv_w[0], *conv_vecs, seq)
        lat1, h2_words, qp = _merge(xg, y_ret, u_conv, proj, w_ret_b, w_conv_b, w_out_b, w_q_b, gain[1:2],
                                    gain[2:3], g1[bs], sh2[bs], sc2[bs], seq)
        piece_idx, gates = _route(qp, keys_b)
        after = gates[:8]
        peer_out = _peer_experts_sc(u_pieces, v_pieces, h2_words, gates, piece_idx)
        out = _final(lat1, peer_out, gain[3:4], g2[bs], out, row0, n, seq)
    return out.reshape(batch, seq, d)
```

```python
import functools

import jax
import jax.numpy as jnp
from jax import lax
from jax.experimental import pallas as pl
from jax.experimental.pallas import tpu as pltpu
from jax.experimental.pallas import tpu_sc as plsc

F32 = jnp.float32
BF16 = jnp.bfloat16

D_MODEL = 1024
GRID_W = 64
EPS = 1e-6
N_MOD = 6
RET_HEADS = 4
RET_DK = 256
RET_CHUNK = 128
ROPE_BASE = 10000.0
CONV_WIDTH = 31
CONV_HALF = CONV_WIDTH // 2
PEER_HEADS = 8
PEER_N_KEYS = 128
PEER_DK_HALF = 128
PEER_TOPK = 16
PEER_SEL = PEER_HEADS * PEER_TOPK

SC_CORES = 2
SC_SUBCORES = 16
SC_WORKERS = SC_CORES * SC_SUBCORES
SC_LANES = 16
SC_ROWS = 32
SC_TB = 16
LANE = 128
PACK_BLOCK = 2 * LANE
PIECES = D_MODEL // PACK_BLOCK
HIGH_HALF = -65536

BATCH_GROUPS = (1, 1, 2, 3, 4, 5)

VMEM_LIMIT = 48 * 1024 * 1024


def _cparams(*sem):
    return pltpu.CompilerParams(dimension_semantics=sem, vmem_limit_bytes=VMEM_LIMIT)


def _sigmoid(x):
    return 1.0 / (1.0 + jnp.exp(-x))


def _silu(x):
    return x * _sigmoid(x)


def _softplus(x):
    return jnp.maximum(x, 0.0) + jnp.log1p(jnp.exp(-jnp.abs(x)))


def _rms(x, gain):
    return x * lax.rsqrt(jnp.mean(x * x, axis=-1, keepdims=True) + EPS) * gain


def _bf16_bits_high(x):
    return lax.bitcast_convert_type(x.astype(BF16).astype(F32), jnp.int32)


def _pack_bf16_pair(lo, hi):
    return _bf16_bits_high(hi) | lax.shift_right_logical(_bf16_bits_high(lo), 16)


def _pack_rows(x):
    parts = [_pack_bf16_pair(x[:, b:b + LANE], x[:, b + LANE:b + PACK_BLOCK])
             for b in range(0, x.shape[1], PACK_BLOCK)]
    return jnp.concatenate(parts, axis=1)


def _mod_kernel(c_ref, w_ref, b_ref, o_ref):
    a = _silu(c_ref[...])
    o_ref[...] = jnp.dot(a, w_ref[...], preferred_element_type=F32,
                         precision=lax.Precision.HIGHEST) + b_ref[...]


def _modulation(c_all, w_mod, b_mod):
    rows, d = c_all.shape
    n = w_mod.shape[1]
    return pl.pallas_call(
        _mod_kernel,
        grid=(n // d,),
        in_specs=[pl.BlockSpec((rows, d), lambda j: (0, 0)),
                  pl.BlockSpec((d, d), lambda j: (0, j)),
                  pl.BlockSpec((1, d), lambda j: (0, j))],
        out_specs=pl.BlockSpec((rows, d), lambda j: (0, j)),
        out_shape=jax.ShapeDtypeStruct((rows, n), F32),
        compiler_params=_cparams("arbitrary"),
        name="modulation",
    )(c_all, w_mod, b_mod)


def _ctx_kernel(raw_ref, ctx_ref, gain_ref, sh_ref, sc_ref, wkv_ref, sf_ref, sb_ref):
    x = ctx_ref[0]
    lc = x.shape[0]
    hc = _rms(x, gain_ref[...]) * (1.0 + sc_ref[...]) + sh_ref[...]
    kv = jnp.dot(hc.astype(BF16), wkv_ref[...], preferred_element_type=F32)
    pos = lax.broadcasted_iota(jnp.int32, (lc, RET_DK), 0).astype(F32)
    tn = (((0,), (0,)), ((), ()))
    for h in range(RET_HEADS):
        lgf = -_softplus(jnp.full((lc, RET_DK), raw_ref[0, h], F32))
        lgb = -_softplus(jnp.full((lc, RET_DK), raw_ref[1, h], F32))
        k = kv[:, h * RET_DK:(h + 1) * RET_DK] * (RET_DK ** -0.5)
        v = kv[:, (RET_HEADS + h) * RET_DK:(RET_HEADS + h + 1) * RET_DK].astype(BF16)
        kf = (k * jnp.exp(lgf * (lc - 1.0 - pos))).astype(BF16)
        kb = (k * jnp.exp(lgb * pos)).astype(BF16)
        sf_ref[0, h] = lax.dot_general(kf, v, tn, preferred_element_type=F32)
        sb_ref[0, h] = lax.dot_general(kb, v, tn, preferred_element_type=F32)


def _context_states(raw, ctx, gain0, csh, csc, w_kv):
    b, lc, d = ctx.shape
    st = jax.ShapeDtypeStruct((b, RET_HEADS, RET_DK, RET_DK), F32)
    st_spec = pl.BlockSpec((1, RET_HEADS, RET_DK, RET_DK), lambda i: (i, 0, 0, 0))
    vec = pl.BlockSpec((1, d), lambda i: (0, 0))
    return pl.pallas_call(
        _ctx_kernel,
        grid=(b,),
        in_specs=[pl.BlockSpec(memory_space=pltpu.SMEM),
                  pl.BlockSpec((1, lc, d), lambda i: (i, 0, 0)),
                  vec, vec, vec,
                  pl.BlockSpec(w_kv.shape, lambda i: (0, 0))],
        out_specs=[st_spec, st_spec],
        out_shape=[st, st],
        compiler_params=_cparams("arbitrary"),
        name="context_states",
    )(raw, ctx, gain0, csh, csc, w_kv)


IN_TM = 1024
IN_TILES = 8
IN_OUT_TILES = 7


def _inproj_kernel(x_ref, gain_ref, sh_ref, sc_ref, cos_ref, sin_ref, w_ref, after_ref, o_ref, xn_ref):
    del after_ref
    j = pl.program_id(1)

    @pl.when(j == 0)
    def _():
        h = _rms(x_ref[...], gain_ref[...]) * (1.0 + sc_ref[0]) + sh_ref[0]
        xn_ref[...] = h.astype(BF16)

    acc = jnp.dot(xn_ref[...], w_ref[...], preferred_element_type=F32)
    half = RET_DK // 2

    def rotary(scale):
        cos = cos_ref[...]
        sin = sin_ref[...]
        for h in range(RET_HEADS):
            t1 = acc[:, h * RET_DK:h * RET_DK + half]
            t2 = acc[:, h * RET_DK + half:(h + 1) * RET_DK]
            o_ref[:, h * RET_DK:h * RET_DK + half] = ((t1 * cos - t2 * sin) * scale).astype(BF16)
            o_ref[:, h * RET_DK + half:(h + 1) * RET_DK] = ((t1 * sin + t2 * cos) * scale).astype(BF16)

    @pl.when(j == 0)
    def _():
        rotary(1.0)

    @pl.when(j == 1)
    def _():
        rotary(RET_DK ** -0.5)

    @pl.when(j == 2)
    def _():
        o_ref[...] = acc.astype(BF16)

    @pl.when(j == 3)
    def _():
        o_ref[...] = _silu(acc).astype(BF16)

    hw = D_MODEL // 2

    @pl.when(j == 4)
    def _():
        o_ref[:, :hw] = (acc[:, :hw] * _sigmoid(acc[:, hw:])).astype(BF16)

    @pl.when(j == 5)
    def _():
        o_ref[:, hw:] = (acc[:, :hw] * _sigmoid(acc[:, hw:])).astype(BF16)

    @pl.when(j >= 6)
    def _():
        o_ref[...] = _sigmoid(acc).astype(BF16)


def _out_tile(j):
    return jnp.where(j <= 4, j, j - 1)


def _in_projection(x2, gain0, sh1, sc1, cos, sin, w_perm, after, seq):
    n, d = x2.shape
    tiles_per_seq = seq // IN_TM
    return pl.pallas_call(
        _inproj_kernel,
        grid=(n // IN_TM, IN_TILES),
        in_specs=[pl.BlockSpec((IN_TM, d), lambda i, j: (i, 0)),
                  pl.BlockSpec((1, d), lambda i, j: (0, 0)),
                  pl.BlockSpec((1, 1, d), lambda i, j: (i // tiles_per_seq, 0, 0)),
                  pl.BlockSpec((1, 1, d), lambda i, j: (i // tiles_per_seq, 0, 0)),
                  pl.BlockSpec((IN_TM, RET_DK // 2), lambda i, j: (i % tiles_per_seq, 0)),
                  pl.BlockSpec((IN_TM, RET_DK // 2), lambda i, j: (i % tiles_per_seq, 0)),
                  pl.BlockSpec((d, d), lambda i, j: (0, j)),
                  pl.BlockSpec(after.shape, lambda i, j: (0, 0))],
        out_specs=pl.BlockSpec((IN_TM, d), lambda i, j: (i, _out_tile(j))),
        out_shape=jax.ShapeDtypeStruct((n, IN_OUT_TILES * d), BF16),
        scratch_shapes=[pltpu.VMEM((IN_TM, d), BF16)],
        compiler_params=_cparams("arbitrary", "arbitrary"),
        name="in_projection",
    )(x2, gain0, sh1, sc1, cos, sin, w_perm, after)


def _ret_kernel(raw_ref, s0f_ref, s0b_ref, q_ref, k_ref, v_ref, g_ref, o_ref, sf_ref, sb_ref, y_ref):
    hd = pl.program_id(1)
    c = RET_CHUNK
    seq = q_ref.shape[0]
    nc = seq // c

    def log_gamma(direction, shape):
        return -_softplus(jnp.full(shape, raw_ref[direction, hd], F32))

    ri = lax.broadcasted_iota(jnp.int32, (c, c), 0).astype(F32)
    ci = lax.broadcasted_iota(jnp.int32, (c, c), 1).astype(F32)
    dec_f = jnp.where(ri >= ci, jnp.exp(log_gamma(0, (c, c)) * jnp.maximum(ri - ci, 0.0)), 0.0)
    dec_b = jnp.where(ci >= ri, jnp.exp(log_gamma(1, (c, c)) * jnp.maximum(ci - ri, 0.0)), 0.0)
    pos = lax.broadcasted_iota(jnp.int32, (c, RET_DK), 0).astype(F32)
    lgf = log_gamma(0, (c, RET_DK))
    lgb = log_gamma(1, (c, RET_DK))
    qdec_f = jnp.exp(lgf * (pos + 1.0))
    kdec_f = jnp.exp(lgf * (c - 1.0 - pos))
    qdec_b = jnp.exp(lgb * (c - pos))
    kdec_b = jnp.exp(lgb * pos)
    cdec_f = jnp.exp(log_gamma(0, (1, RET_DK)) * c)
    cdec_b = jnp.exp(log_gamma(1, (1, RET_DK)) * c)
    sf_ref[...] = s0f_ref[0, 0]
    sb_ref[...] = s0b_ref[0, 0]
    nt = (((1,), (1,)), ((), ()))
    tn = (((0,), (0,)), ((), ()))

    def chunk(st_ref, row, dec, qdec, kdec, cdec):
        q = q_ref[pl.ds(row, c), :]
        k = k_ref[pl.ds(row, c), :]
        v = v_ref[pl.ds(row, c), :]
        s = lax.dot_general(q, k, nt, preferred_element_type=F32) * dec
        st = st_ref[...]
        y = jnp.dot(s.astype(BF16), v, preferred_element_type=F32)
        y = y + jnp.dot((q.astype(F32) * qdec).astype(BF16), st.astype(BF16), preferred_element_type=F32)
        kd = (k.astype(F32) * kdec).astype(BF16)
        st_ref[...] = st * cdec + lax.dot_general(kd, v, tn, preferred_element_type=F32)
        return y

    def first_half(i, carry):
        rf = pl.multiple_of(i * c, c)
        rb = pl.multiple_of((nc - 1 - i) * c, c)
        y_ref[pl.ds(rf, c), :] = chunk(sf_ref, rf, dec_f, qdec_f, kdec_f, cdec_f)
        y_ref[pl.ds(rb, c), :] = chunk(sb_ref, rb, dec_b, qdec_b, kdec_b, cdec_b)
        return carry

    def second_half(i, carry):
        rf = pl.multiple_of(i * c, c)
        rb = pl.multiple_of((nc - 1 - i) * c, c)
        y_ref[pl.ds(rf, c), :] += chunk(sf_ref, rf, dec_f, qdec_f, kdec_f, cdec_f)
        y_ref[pl.ds(rb, c), :] += chunk(sb_ref, rb, dec_b, qdec_b, kdec_b, cdec_b)
        return carry

    lax.fori_loop(0, nc // 2, first_half, 0)
    lax.fori_loop(nc // 2, nc, second_half, 0)

    def finish(i, carry):
        r = pl.multiple_of(i * c, c)
        y = y_ref[pl.ds(r, c), :]
        yn = y * lax.rsqrt(jnp.mean(y * y, axis=-1, keepdims=True) + EPS)
        o_ref[pl.ds(r, c), :] = (yn * g_ref[pl.ds(r, c), :].astype(F32)).astype(BF16)
        return carry

    lax.fori_loop(0, nc, finish, 0)


def _retention(raw, st_f, st_b, proj, batch, seq):
    n = batch * seq
    st_spec = pl.BlockSpec((1, 1, RET_DK, RET_DK), lambda b, h: (b, h, 0, 0))

    def col(tile):
        return pl.BlockSpec((seq, RET_DK), lambda b, h: (b, tile * RET_HEADS + h))

    return pl.pallas_call(
        _ret_kernel,
        grid=(batch, RET_HEADS),
        in_specs=[pl.BlockSpec(memory_space=pltpu.SMEM), st_spec, st_spec,
                  col(0), col(1), col(2), col(3)],
        out_specs=pl.BlockSpec((seq, RET_DK), lambda b, h: (b, h)),
        out_shape=jax.ShapeDtypeStruct((n, RET_HEADS * RET_DK), BF16),
        scratch_shapes=[pltpu.VMEM((RET_DK, RET_DK), F32), pltpu.VMEM((RET_DK, RET_DK), F32),
                        pltpu.VMEM((seq, RET_DK), F32)],
        compiler_params=_cparams("arbitrary", "arbitrary"),
        name="retention",
    )(raw, st_f, st_b, proj, proj, proj, proj)


CONV_TL = 256
CONV_HALO = 16
CONV_RB = 32


def _conv_kernel(prev_ref, cur_ref, next_ref, cw_ref, cb_ref, lg_ref, lb_ref, o_ref, pad_ref, *, seq):
    i = pl.program_id(0)
    tl = CONV_TL
    has_prev = (i * tl) % seq != 0
    has_next = ((i + 1) * tl) % seq != 0
    pad_ref[0:CONV_HALO, :] = jnp.where(has_prev, prev_ref[...].astype(F32), 0.0)
    pad_ref[CONV_HALO:CONV_HALO + tl, :] = cur_ref[...].astype(F32)
    pad_ref[CONV_HALO + tl:, :] = jnp.where(has_next, next_ref[...].astype(F32), 0.0)
    off = CONV_HALO - CONV_HALF
    for rb in range(tl // CONV_RB):
        base = rb * CONV_RB + off
        acc = pad_ref[base:base + CONV_RB, :] * cw_ref[0:1, :]
        for w in range(1, CONV_WIDTH):
            acc = acc + pad_ref[base + w:base + w + CONV_RB, :] * cw_ref[w:w + 1, :]
        u = acc + cb_ref[...]
        mu = jnp.mean(u, axis=-1, keepdims=True)
        uc = u - mu
        y = uc * lax.rsqrt(jnp.mean(uc * uc, axis=-1, keepdims=True) + EPS) * lg_ref[...] + lb_ref[...]
        o_ref[rb * CONV_RB:(rb + 1) * CONV_RB, :] = _silu(y).astype(BF16)


def _conv_branch(proj, conv_w, conv_b, ln_g, ln_b, seq):
    n = proj.shape[0]
    d = D_MODEL
    glu_tile = 4
    hb = CONV_TL // CONV_HALO
    n_halo = n // CONV_HALO
    vec = pl.BlockSpec((1, d), lambda i: (0, 0))
    return pl.pallas_call(
        functools.partial(_conv_kernel, seq=seq),
        grid=(n // CONV_TL,),
        in_specs=[pl.BlockSpec((CONV_HALO, d), lambda i: (jnp.maximum(i * hb - 1, 0), glu_tile)),
                  pl.BlockSpec((CONV_TL, d), lambda i: (i, glu_tile)),
                  pl.BlockSpec((CONV_HALO, d), lambda i: (jnp.minimum((i + 1) * hb, n_halo - 1), glu_tile)),
                  pl.BlockSpec((CONV_WIDTH, d), lambda i: (0, 0)),
                  vec, vec, vec],
        out_specs=pl.BlockSpec((CONV_TL, d), lambda i: (i, 0)),
        out_shape=jax.ShapeDtypeStruct((n, d), BF16),
        scratch_shapes=[pltpu.VMEM((CONV_TL + 2 * CONV_HALO, d), F32)],
        compiler_params=_cparams("arbitrary"),
        name="conv_branch",
    )(proj, proj, proj, conv_w, conv_b, ln_g, ln_b)


MERGE_TM = 256


def _merge_kernel(x_ref, yr_ref, uc_ref, ga_ref, gb_ref, wr_ref, wc_ref, wo_ref, wq_ref,
                  gain1_ref, gain2_ref, g1_ref, sh2_ref, sc2_ref, lat_ref, h2_ref, qp_ref):
    a = jnp.dot(yr_ref[...], wr_ref[...], preferred_element_type=F32)
    b = jnp.dot(uc_ref[...], wc_ref[...], preferred_element_type=F32)
    merged = ga_ref[...].astype(F32) * a + gb_ref[...].astype(F32) * b
    y = jnp.dot(merged.astype(BF16), wo_ref[...], preferred_element_type=F32)
    lat = x_ref[...] + g1_ref[0] * _rms(y, gain1_ref[...])
    lat_ref[...] = lat
    h2 = _rms(lat, gain2_ref[...]) * (1.0 + sc2_ref[0]) + sh2_ref[0]
    h2_ref[...] = _pack_rows(h2)
    qp_ref[...] = jnp.dot(h2.astype(BF16), wq_ref[...], preferred_element_type=F32).astype(BF16)


def _merge(x2, y_ret, u_conv, proj, w_ret_o, w_conv_o, w_out, w_q, gain1, gain2, g1, sh2, sc2, seq):
    n, d = x2.shape
    tps = seq // MERGE_TM
    nq = w_q.shape[1]
    row = pl.BlockSpec((MERGE_TM, d), lambda i: (i, 0))
    wsp = pl.BlockSpec((d, d), lambda i: (0, 0))
    vec = pl.BlockSpec((1, d), lambda i: (0, 0))
    bvec = pl.BlockSpec((1, 1, d), lambda i: (i // tps, 0, 0))
    return pl.pallas_call(
        _merge_kernel,
        grid=(n // MERGE_TM,),
        in_specs=[row, row, row,
                  pl.BlockSpec((MERGE_TM, d), lambda i: (i, 5)),
                  pl.BlockSpec((MERGE_TM, d), lambda i: (i, 6)),
                  wsp, wsp, wsp,
                  pl.BlockSpec((d, nq), lambda i: (0, 0)),
                  vec, vec, bvec, bvec, bvec],
        out_specs=[row, pl.BlockSpec((MERGE_TM, d // 2), lambda i: (i, 0)),
                   pl.BlockSpec((MERGE_TM, nq), lambda i: (i, 0))],
        out_shape=[jax.ShapeDtypeStruct((n, d), F32), jax.ShapeDtypeStruct((n, d // 2), jnp.int32),
                   jax.ShapeDtypeStruct((n, nq), BF16)],
        compiler_params=_cparams("arbitrary"),
        name="merge",
    )(x2, y_ret, u_conv, proj, proj, w_ret_o, w_conv_o, w_out, w_q, gain1, gain2, g1, sh2, sc2)


ROUTE_T = 512


def _topk_rows(s, k, payload=None):
    r = s.shape[0]
    rows = lax.broadcasted_iota(jnp.int32, s.shape, 0).astype(F32)
    vals, idxs = [], []
    for _ in range(k):
        m = jnp.max(s, axis=0, keepdims=True)
        pos = jnp.min(jnp.where(s == m, rows, float(r)), axis=0, keepdims=True)
        hit = rows == pos
        vals.append(m)
        if payload is None:
            idxs.append(pos)
        else:
            idxs.append(jnp.max(jnp.where(hit, payload, -1.0), axis=0, keepdims=True))
        s = jnp.where(hit, -jnp.inf, s)
    return jnp.concatenate(vals, axis=0), jnp.concatenate(idxs, axis=0)


def _route_kernel(q_ref, keys_ref, e_ref, g_ref, es_ref, gs_ref):
    h = pl.program_id(1)
    nt = (((1,), (1,)), ((), ()))
    row0 = pl.multiple_of(h * PEER_TOPK, PEER_TOPK)
    for t in range(ROUTE_T // LANE):
        cols = slice(t * LANE, (t + 1) * LANE)
        tops = []
        for p in range(2):
            qh = q_ref[cols, p * PEER_DK_HALF:(p + 1) * PEER_DK_HALF]
            s = lax.dot_general(keys_ref[0, p], qh, nt, preferred_element_type=F32)
            tops.append(_topk_rows(s, PEER_TOPK))
        (v1, i1), (v2, i2) = tops
        widths = [PEER_TOPK // (a + 1) for a in range(PEER_TOPK)]
        pad = (-sum(widths)) % 8
        cand = jnp.concatenate([v1[a:a + 1] + v2[:w] for a, w in enumerate(widths)]
                               + [jnp.full((pad, LANE), -jnp.inf, F32)], axis=0)
        cidx = jnp.concatenate([i1[a:a + 1] * float(PEER_N_KEYS) + i2[:w] for a, w in enumerate(widths)]
                               + [jnp.zeros((pad, LANE), F32)], axis=0)
        best, experts = _topk_rows(cand, PEER_TOPK, payload=cidx)
        ex = jnp.exp(best - best[0:1])
        gates = ex / jnp.sum(ex, axis=0, keepdims=True)
        es_ref[pl.ds(row0, PEER_TOPK), cols] = experts.astype(jnp.int32)
        gs_ref[pl.ds(row0, PEER_TOPK), cols] = gates

    @pl.when(h == PEER_HEADS - 1)
    def _():
        first_piece = es_ref[...].T * PIECES
        for s in range(PIECES):
            e_ref[:, s * PEER_SEL:(s + 1) * PEER_SEL] = first_piece + s
        g_ref[...] = gs_ref[...].T


def _route(qp, keys):
    n = qp.shape[0]
    hw = 2 * PEER_DK_HALF
    out = pl.BlockSpec((ROUTE_T, PEER_SEL), lambda i, h: (i, 0))
    return pl.pallas_call(
        _route_kernel,
        grid=(n // ROUTE_T, PEER_HEADS),
        in_specs=[pl.BlockSpec((ROUTE_T, hw), lambda i, h: (i, h)),
                  pl.BlockSpec((1, 2, PEER_N_KEYS, PEER_DK_HALF), lambda i, h: (h, 0, 0, 0))],
        out_specs=[pl.BlockSpec((ROUTE_T, PIECES * PEER_SEL), lambda i, h: (i, 0)), out],
        out_shape=[jax.ShapeDtypeStruct((n, PIECES * PEER_SEL), jnp.int32),
                   jax.ShapeDtypeStruct((n, PEER_SEL), F32)],
        scratch_shapes=[pltpu.VMEM((PEER_SEL, ROUTE_T), jnp.int32), pltpu.VMEM((PEER_SEL, ROUTE_T), F32)],
        compiler_params=_cparams("arbitrary", "arbitrary"),
        name="peer_route",
    )(qp, keys)


def _sc_mesh():
    return plsc.VectorSubcoreMesh(core_axis_name="c", subcore_axis_name="s")


def _worker_id():
    return lax.axis_index("s") * SC_CORES + lax.axis_index("c")


SC_NBUF = PEER_SEL // SC_ROWS
SC_QUAD = 4
SC_GROUP = 2
SC_WORDS = LANE // SC_LANES


def _sc_scratch():
    assert SC_NBUF == PIECES
    return [pltpu.VMEM((SC_NBUF, PIECES, SC_ROWS, LANE), jnp.int32)] + [pltpu.SemaphoreType.DMA] * SC_NBUF


def _gather(tab_hbm, idx_v, rows, sem, tl, ch, s):
    return pltpu.make_async_copy(tab_hbm.at[idx_v.at[tl, pl.ds(s * PEER_SEL + ch * SC_ROWS, SC_ROWS)]],
                                 rows.at[ch, s], sem)


def _widen_pair(packed_bf16):
    words = plsc.bitcast(packed_bf16, jnp.int32)
    return (plsc.bitcast(lax.shift_left(words, 16), F32), plsc.bitcast(words & HIGH_HALF, F32))


def _gelu_tanh_via_exp(x):
    z = 0.7978845608028654 * (x + 0.044715 * (x * x * x))
    return 0.5 * x * (2.0 - 2.0 / (jnp.exp(2.0 * z) + 1.0))


def _peer_experts_sc(u_pieces, v_pieces, xw, gates, piece_idx):
    n = xw.shape[0]
    per = n // SC_WORKERS
    d = PIECES * PACK_BLOCK

    @functools.partial(
        pl.kernel, mesh=_sc_mesh(),
        out_type=jax.ShapeDtypeStruct((n, d), F32),
        scratch_types=[pltpu.VMEM((2, SC_TB, PIECES * PEER_SEL), jnp.int32),
                       pltpu.VMEM((SC_TB, PIECES * LANE), jnp.int32),
                       pltpu.VMEM((SC_TB, PEER_SEL), F32), pltpu.VMEM((SC_ROWS, SC_LANES), F32),
                       pltpu.VMEM((SC_TB, PEER_SEL), F32), pltpu.VMEM((SC_TB, PEER_SEL), jnp.int32),
                       pltpu.VMEM((SC_TB, d), F32)] + _sc_scratch() + [pltpu.SemaphoreType.DMA] * (PIECES + 2),
        compiler_params=pltpu.CompilerParams(needs_layout_passes=False),
        name="peer_experts_sc",
    )
    def run(u_hbm, v_hbm, x_hbm, g_hbm, idx_hbm, out_hbm, idx2_v, x_v, res_v, part_v, g_v, w_v, acc_v,
            rows_all, *all_sems):
        sems, sems_v = all_sems[:SC_NBUF], all_sems[SC_NBUF:SC_NBUF + PIECES]
        sem_idx, sem_out = all_sems[SC_NBUF + PIECES:]
        base = _worker_id() * per
        nblk = per // SC_TB

        def out_copy(t0):
            return pltpu.make_async_copy(acc_v, out_hbm.at[pl.ds(t0, SC_TB)], sem_out)
        lane = lax.iota(jnp.int32, SC_LANES)
        zero = jnp.zeros((SC_LANES,), F32)
        quads_per_chunk = SC_ROWS // SC_QUAD

        def first_score_gathers(idx_v, s):
            return [_gather(u_hbm, idx_v, rows_all, sems[ch], 0, ch, s) for ch in range(SC_NBUF)]

        pltpu.sync_copy(idx_hbm.at[pl.ds(base, SC_TB)], idx2_v.at[0])
        for s in range(PIECES):
            for cp in first_score_gathers(idx2_v.at[0], s):
                cp.start()

        @pl.loop(0, nblk)
        def _(blk):
            tok0 = pl.multiple_of(base + blk * SC_TB, SC_TB)
            idx_v = idx2_v.at[blk & 1]
            idx_next = idx2_v.at[1 - (blk & 1)]
            has_next = blk + 1 < nblk

            def gathers(tl, ch):
                return [_gather(u_hbm, idx_v, rows_all, sems[ch], tl, ch, s) for s in range(PIECES)]

            def gathers_v(tl, s):
                return [_gather(v_hbm, idx_v, rows_all, sems_v[s], tl, ch, s) for ch in range(SC_NBUF)]

            pltpu.sync_copy(x_hbm.at[pl.ds(tok0, SC_TB)], x_v)
            pltpu.sync_copy(g_hbm.at[pl.ds(tok0, SC_TB)], g_v)

            def next_idx_copy():
                return pltpu.make_async_copy(idx_hbm.at[pl.ds(tok0 + SC_TB, SC_TB)], idx_next, sem_idx)

            @pl.when(has_next)
            def _():
                next_idx_copy().start()

            @pl.loop(0, SC_TB)
            def _(tl):
                xs = [plsc.bitcast(x_v[tl, pl.ds(i * SC_LANES, SC_LANES)], BF16) for i in range(PIECES * SC_WORDS)]
                for ch in range(SC_NBUF):
                    for cp in gathers(tl, ch):
                        cp.wait()
                    rows = rows_all.at[ch]

                    @pl.loop(0, SC_ROWS, step=SC_GROUP)
                    def _(r0):
                        for r in range(SC_GROUP):
                            acc = zero
                            for s in range(PIECES):
                                for w0 in range(0, SC_WORDS, SC_QUAD):
                                    part = None
                                    for j in range(w0, w0 + SC_QUAD):
                                        prod = plsc.bitcast(rows[s, r0 + r, pl.ds(j * SC_LANES, SC_LANES)],
                                                            BF16) * xs[s * SC_WORDS + j]
                                        part = prod if part is None else part + prod
                                    lo, hi = _widen_pair(part)
                                    acc = acc + lo + hi
                            part_v[r0 + r, :] = acc
                    for g in range(SC_ROWS // SC_LANES):
                        row_ids = lane + g * SC_LANES
                        res = zero
                        for col in range(SC_LANES):
                            res = res + plsc.load_gather(part_v, [row_ids, jnp.full((SC_LANES,), col, jnp.int32)])
                        res_v[tl, pl.ds(ch * SC_ROWS + g * SC_LANES, SC_LANES)] = res

                    @pl.when(tl + 1 < SC_TB)
                    def _():
                        for cp in gathers(tl + 1, ch):
                            cp.start()

                    @pl.when(tl + 1 == SC_TB)
                    def _():
                        for s in range(PIECES):
                            _gather(v_hbm, idx_v, rows_all, sems_v[s], 0, ch, s).start()

            @pl.when(blk > 0)
            def _():
                out_copy(tok0).wait()

            @pl.loop(0, SC_TB)
            def _(t):
                for c in range(0, PEER_SEL, SC_LANES):
                    w = g_v[t, pl.ds(c, SC_LANES)] * _gelu_tanh_via_exp(res_v[t, pl.ds(c, SC_LANES)])
                    w_v[t, pl.ds(c, SC_LANES)] = plsc.bitcast(plsc.pack(w, w, format=plsc.PackFormat.INTERLEAVED),
                                                             jnp.int32)

            @pl.loop(0, SC_TB)
            def _(tl):
                tl_vec = jnp.full((SC_LANES,), 0, jnp.int32) + tl
                for s in range(PIECES):
                    for cp in gathers_v(tl, s):
                        cp.wait()

                    def row_quad(q, accs):
                        ch = lax.shift_right_logical(q, quads_per_chunk.bit_length() - 1)
                        r0 = pl.multiple_of((q & (quads_per_chunk - 1)) * SC_QUAD, SC_QUAD)
                        quad = rows_all.at[ch, s, pl.ds(r0, SC_QUAD)]
                        col0 = jnp.full((SC_LANES,), 0, jnp.int32) + q * SC_QUAD
                        ws = [plsc.bitcast(plsc.load_gather(w_v, [tl_vec, col0 + j]), BF16) for j in range(SC_QUAD)]
                        accs = list(accs)
                        for k in range(SC_WORDS):
                            part = None
                            for j in range(SC_QUAD):
                                prod = plsc.bitcast(quad[j, pl.ds(k * SC_LANES, SC_LANES)], BF16) * ws[j]
                                part = prod if part is None else part + prod
                            lo, hi = _widen_pair(part)
                            accs[k] = accs[k] + lo
                            accs[SC_WORDS + k] = accs[SC_WORDS + k] + hi
                        return tuple(accs)

                    accs = lax.fori_loop(0, PEER_SEL // SC_QUAD, row_quad, (zero,) * (2 * SC_WORDS))
                    for i, acc in enumerate(accs):
                        acc_v[tl, pl.ds(s * PACK_BLOCK + i * SC_LANES, SC_LANES)] = acc

                    @pl.when(tl + 1 < SC_TB)
                    def _():
                        for cp in gathers_v(tl + 1, s):
                            cp.start()

                    @pl.when(jnp.logical_and(tl + 1 == SC_TB, has_next))
                    def _():
                        if s == 0:
                            next_idx_copy().wait()
                        for cp in first_score_gathers(idx_next, s):
                            cp.start()

            out_copy(tok0).start()

        out_copy(base).wait()

    return run(u_pieces, v_pieces, xw, gates, piece_idx)


FIN_TM = 1024


def _final_kernel(lat_ref, p_ref, gain_ref, g2_ref, *rest):
    o_ref = rest[-1]
    o_ref[...] = lat_ref[...] + g2_ref[0] * _rms(p_ref[...], gain_ref[...])


def _final(lat, peer, gain3, g2, out_all, row0, n_total, seq):
    n, d = lat.shape
    tps = seq // FIN_TM
    row = pl.BlockSpec((FIN_TM, d), lambda i: (i, 0))
    ins = [lat, peer, gain3, g2] + ([] if out_all is None else [out_all])
    return pl.pallas_call(
        _final_kernel, grid=(n // FIN_TM,),
        in_specs=[row, row, pl.BlockSpec((1, d), lambda i: (0, 0)),
                  pl.BlockSpec((1, 1, d), lambda i: (i // tps, 0, 0))]
        + ([] if out_all is None else [pl.BlockSpec(memory_space=pl.ANY)]),
        out_specs=pl.BlockSpec((FIN_TM, d), lambda i: (i + row0 // FIN_TM, 0)),
        out_shape=jax.ShapeDtypeStruct((n_total, d), F32),
        input_output_aliases={} if out_all is None else {len(ins) - 1: 0},
        compiler_params=_cparams("arbitrary"), name="final_residual",
    )(*ins)


PACK_TE = 512


def _pack_table_kernel(t_ref, o_ref):
    x = t_ref[...]
    for s in range(PIECES):
        b = s * PACK_BLOCK
        o_ref[pl.ds(s, PACK_TE, stride=PIECES), :] = _pack_bf16_pair(x[:, b:b + LANE], x[:, b + LANE:b + PACK_BLOCK])


def _expert_pieces(table):
    e, d = table.shape
    assert e % PACK_TE == 0 and d == PIECES * PACK_BLOCK
    return pl.pallas_call(
        _pack_table_kernel, grid=(e // PACK_TE,),
        in_specs=[pl.BlockSpec((PACK_TE, d), lambda i: (i, 0))],
        out_specs=pl.BlockSpec((PACK_TE * PIECES, LANE), lambda i: (i, 0)),
        out_shape=jax.ShapeDtypeStruct((e * PIECES, LANE), jnp.int32),
        compiler_params=_cparams("arbitrary"), name="pack_experts",
    )(table)


def _axial_rotary(seq):
    n_rows = seq // GRID_W
    rows = jnp.repeat(jnp.arange(n_rows, dtype=F32), GRID_W)
    cols = jnp.tile(jnp.arange(GRID_W, dtype=F32), n_rows)
    quarter = RET_DK // 4
    inv = ROPE_BASE ** (-jnp.arange(quarter, dtype=F32) / quarter)
    ang = jnp.concatenate([rows[:, None] * inv, cols[:, None] * inv], axis=-1)
    return jnp.cos(ang), jnp.sin(ang)


def kernel(x, c, ctx, c_ctx, w_mod, b_mod, norm_gain, w_in, ret_decay_raw, w_ret_o, conv_w, conv_b,
           conv_norm_g, conv_norm_b, w_conv_o, w_out, peer_wq, peer_keys, peer_u, peer_v):
    assert w_mod.shape[0] == 1, "single layer"
    batch, seq, d = x.shape
    n = batch * seq
    assert d == D_MODEL and seq % IN_TM == 0 and seq % (SC_WORKERS * SC_TB) == 0
    gain = norm_gain[0]
    x2 = x.reshape(n, d)

    pad = (-(batch + 1)) % 8
    c_all = jnp.concatenate([c, c_ctx[None, :], jnp.zeros((pad, d), F32)], axis=0)
    mod = _modulation(c_all, w_mod[0], b_mod[0][None, :])
    mod_lat = mod[:batch].reshape(batch, N_MOD, 1, d)
    sh1, sc1, g1, sh2, sc2, g2 = (mod_lat[:, i] for i in range(N_MOD))
    mod_ctx = mod[batch:batch + 1].reshape(N_MOD, 1, d)

    w_in0 = w_in[0]
    q_w = RET_HEADS * RET_DK
    w_kv = w_in0[:, q_w:3 * q_w].astype(BF16)
    st_f, st_b = _context_states(ret_decay_raw[0], ctx, gain[0:1], mod_ctx[0], mod_ctx[1], w_kv)

    hw = d // 2
    glu_a = w_in0[:, 4 * d:5 * d]
    glu_b = w_in0[:, 5 * d:6 * d]
    w_perm = jnp.concatenate([w_in0[:, :4 * d], glu_a[:, :hw], glu_b[:, :hw], glu_a[:, hw:], glu_b[:, hw:],
                              w_in0[:, 6 * d:]], axis=1).astype(BF16)
    cos, sin = _axial_rotary(seq)
    w_ret_b, w_conv_b, w_out_b, w_q_b = (w[0].astype(BF16) for w in (w_ret_o, w_conv_o, w_out, peer_wq))
    keys_b = peer_keys[0].astype(BF16)
    u_pieces = _expert_pieces(peer_u[0])
    v_pieces = _expert_pieces(peer_v[0])
    conv_vecs = (conv_b[0][None, :], conv_norm_g[0][None, :], conv_norm_b[0][None, :])

    groups = BATCH_GROUPS if sum(BATCH_GROUPS) == batch else (batch,)
    for tile_rows in (IN_TM, CONV_TL, MERGE_TM, ROUTE_T, FIN_TM, SC_WORKERS * SC_TB):
        assert seq % tile_rows == 0, "token count per sequence must be a multiple of every row tile"
    out = None
    b0 = 0
    after = jnp.zeros((8, PEER_SEL), F32)
    for gb in groups:
        bs = slice(b0, b0 + gb)
        row0 = b0 * seq
        xg = x2[row0:row0 + gb * seq]
        b0 += gb
        proj = _in_projection(xg, gain[0:1], sh1[bs], sc1[bs], cos, sin, w_perm, after, seq)
        y_ret = _retention(ret_decay_raw[0], st_f[bs], st_b[bs], proj, gb, seq)
        u_conv = _conv_branch(proj, conv_w[0], *conv_vecs, seq)
        lat1, h2_words, qp = _merge(xg, y_ret, u_conv, proj, w_ret_b, w_conv_b, w_out_b, w_q_b, gain[1:2],
                                    gain[2:3], g1[bs], sh2[bs], sc2[bs], seq)
        piece_idx, gates = _route(qp, keys_b)
        after = gates[:8]
        peer_out = _peer_experts_sc(u_pieces, v_pieces, h2_words, gates, piece_idx)
        out = _final(lat1, peer_out, gain[3:4], g2[bs], out, row0, n, seq)
    return out.reshape(batch, seq, d)
```
